```python
import jax, jax.numpy as jnp
from jax import lax
import numpy as np

D_MODEL = 2048
BATCH = 1
SEQ = 8192
DEPTH = 1

M_HEADS = 8
M_DQK = 64
M_DV = 128
M_CONV = 4
M_CHUNK = 64
F_HEADS = 8
F_DH = 128
F_BLOCK = 128
N_EXPERTS = 32
TOP_K = 4
D_FF = 2048
SWIGLU_LIMIT = 7.0
SWIGLU_ALPHA = 1.702
EPS = 1e-6

M_QK_W = M_HEADS * M_DQK
M_V_W = M_HEADS * M_DV
F_W = F_HEADS * F_DH
IN_SIZES = (M_QK_W, M_QK_W, M_V_W, M_HEADS, M_HEADS, M_V_W,
            F_W, F_W, F_W, F_HEADS, F_W, D_MODEL, D_MODEL)
IN_WIDTH = sum(IN_SIZES)

kernel_name = "hybrid_mlstm_fox_moe_adaln"


def rmsnorm(x, g):
    xf = x.astype(jnp.float32)
    y = xf * lax.rsqrt(jnp.mean(xf * xf, axis=-1, keepdims=True) + EPS)
    return (y * g.astype(jnp.float32)).astype(x.dtype)


def causal_dwconv(x, w, b):
    C = x.shape[-1]
    y = lax.conv_general_dilated(x, w[:, None, :].astype(x.dtype), window_strides=(1,),
                                 padding=[(w.shape[0] - 1, 0)],
                                 dimension_numbers=('NWC', 'WIO', 'NWC'),
                                 feature_group_count=C)
    return y + b


def to_heads(a, h):
    B, S, W = a.shape
    return a.reshape(B, S, h, W // h).transpose(0, 2, 1, 3)


def mlstm_chunkwise(q, k, v, i_pre, logf):
    B, H, S, dqk = q.shape
    dv = v.shape[-1]
    L = M_CHUNK
    nc = S // L

    def to_chunks(a):
        return jnp.moveaxis(a.reshape(B, H, nc, L, *a.shape[3:]), 2, 0)

    qc, kc, vc, ic, fc = (to_chunks(a) for a in (q, k * (dqk ** -0.5), v, i_pre, logf))
    causal = jnp.tril(jnp.ones((L, L), bool))

    def step(carry, inp):
        C, n, m = carry
        qb, kb, vb, ib, fb = inp
        b = jnp.cumsum(fb, axis=-1)
        g = b[..., -1]
        dmat = jnp.where(causal, b[..., :, None] - b[..., None, :] + ib[..., None, :], -jnp.inf)
        inter = b + m[..., None]
        m_t = jnp.maximum(inter, jnp.max(dmat, axis=-1))
        w_intra = jnp.exp(dmat - m_t[..., None])
        w_inter = jnp.exp(inter - m_t)
        s = jnp.einsum('bhtd,bhsd->bhts', qb, kb) * w_intra
        num = (jnp.einsum('bhts,bhsv->bhtv', s, vb)
               + w_inter[..., None] * jnp.einsum('bhtd,bhdv->bhtv', qb, C))
        den = jnp.sum(s, axis=-1) + w_inter * jnp.einsum('bhtd,bhd->bht', qb, n)
        h = num / jnp.maximum(jnp.abs(den), jnp.exp(-m_t))[..., None]
        a = g[..., None] - b + ib
        m_new = jnp.maximum(g + m, jnp.max(a, axis=-1))
        wk = jnp.exp(a - m_new[..., None])
        decay = jnp.exp(g + m - m_new)
        C_new = decay[..., None, None] * C + jnp.einsum('bhs,bhsd,bhsv->bhdv', wk, kb, vb)
        n_new = decay[..., None] * n + jnp.einsum('bhs,bhsd->bhd', wk, kb)
        return (C_new, n_new, m_new), h

    init = (jnp.zeros((B, H, dqk, dv), jnp.float32),
            jnp.zeros((B, H, dqk), jnp.float32),
            jnp.zeros((B, H), jnp.float32))
    _, hs = lax.scan(step, init, (qc, kc, vc, ic, fc))
    return jnp.moveaxis(hs, 0, 2).reshape(B, H, S, dv)


def forgetting_attention(q, k, v, logf):
    B, H, S, dh = q.shape
    nb = S // F_BLOCK
    F = jnp.cumsum(logf, axis=-1)
    kpos = jnp.arange(S)
    scale = dh ** -0.5

    def block(i):
        start = i * F_BLOCK
        qb = lax.dynamic_slice_in_dim(q, start, F_BLOCK, axis=2)
        Fb = lax.dynamic_slice_in_dim(F, start, F_BLOCK, axis=2)
        qpos = start + jnp.arange(F_BLOCK)
        logits = (jnp.einsum('bhqd,bhkd->bhqk', qb, k) * scale
                  + Fb[..., :, None] - F[..., None, :])
        logits = jnp.where(kpos[None, :] <= qpos[:, None], logits, -jnp.inf)
        p = jax.nn.softmax(logits, axis=-1)
        return jnp.einsum('bhqk,bhkd->bhqd', p, v)

    out = lax.map(block, jnp.arange(nb))
    return jnp.moveaxis(out, 0, 2).reshape(B, H, S, dh)


def hybrid_mixer(h, w_in, m_conv_w, m_conv_b, m_i_bias, m_f_bias, m_out_norm_g,
                 f_f_bias, f_q_norm_g, f_k_norm_g, w_branch_m, w_branch_f, w_out):
    f32 = jnp.float32
    B, S, _ = h.shape
    z = (h @ w_in).astype(f32)
    split_idx = np.cumsum(IN_SIZES)[:-1].tolist()
    mq, mk, mv, mi, mf, mo, fq, fk, fv, ff, fo, ga, gb = jnp.split(z, split_idx, axis=-1)

    mqk = jax.nn.silu(causal_dwconv(jnp.concatenate([mq, mk], axis=-1),
                                    m_conv_w.astype(f32), m_conv_b.astype(f32)))
    mq, mk = mqk[..., :M_QK_W], mqk[..., M_QK_W:]
    i_pre = (mi + m_i_bias.astype(f32)).transpose(0, 2, 1)
    logf_m = jax.nn.log_sigmoid(mf + m_f_bias.astype(f32)).transpose(0, 2, 1)
    hm = mlstm_chunkwise(to_heads(mq, M_HEADS), to_heads(mk, M_HEADS),
                         to_heads(mv, M_HEADS), i_pre, logf_m)
    hm = rmsnorm(hm.transpose(0, 2, 1, 3), m_out_norm_g).reshape(B, S, M_V_W)
    hm = hm * jax.nn.sigmoid(mo)

    fq = rmsnorm(fq.reshape(B, S, F_HEADS, F_DH), f_q_norm_g).transpose(0, 2, 1, 3)
    fk = rmsnorm(fk.reshape(B, S, F_HEADS, F_DH), f_k_norm_g).transpose(0, 2, 1, 3)
    logf_f = jax.nn.log_sigmoid(ff + f_f_bias.astype(f32)).transpose(0, 2, 1)
    hf = forgetting_attention(fq, fk, to_heads(fv, F_HEADS), logf_f)
    hf = hf.transpose(0, 2, 1, 3).reshape(B, S, F_W) * jax.nn.sigmoid(fo)

    y = (jax.nn.sigmoid(ga) * (hm @ w_branch_m.astype(f32))
         + jax.nn.sigmoid(gb) * (hf @ w_branch_f.astype(f32)))
    return (y @ w_out.astype(f32)).astype(h.dtype)


def moe_ffn(h, w_router, b_router, w_gate_up, b_gate_up, w_down, b_down):
    B, S, D = h.shape
    t = h.reshape(B * S, D)
    logits = (t @ w_router + b_router).astype(jnp.float32)
    top_val, top_idx = lax.top_k(logits, TOP_K)
    top_w = jax.nn.softmax(top_val, axis=-1)
    dense_w = jnp.einsum('tk,tke->te', top_w,
                         jax.nn.one_hot(top_idx, N_EXPERTS, dtype=jnp.float32))

    def expert(acc, p):
        wgu, bgu, wd, bd, we = p
        gu = t @ wgu + bgu
        gate = jnp.minimum(gu[:, :D_FF], SWIGLU_LIMIT)
        up = jnp.clip(gu[:, D_FF:], -SWIGLU_LIMIT, SWIGLU_LIMIT)
        act = (up + 1.0) * gate * jax.nn.sigmoid(SWIGLU_ALPHA * gate)
        y = act @ wd + bd
        return acc + we[:, None] * y.astype(jnp.float32), None

    acc, _ = lax.scan(expert, jnp.zeros((B * S, D), jnp.float32),
                      (w_gate_up, b_gate_up, w_down, b_down, dense_w.T))
    return acc.astype(h.dtype).reshape(B, S, D)


def setup_inputs(seed: int = 0) -> dict:
    key = jax.random.key(seed)
    ks = jax.random.split(key, 25)
    D, L = D_MODEL, DEPTH

    def nrm(k, shape, s):
        return jax.random.normal(k, shape, jnp.float32) * s

    return {
        "x": nrm(ks[0], (BATCH, SEQ, D), 1.0),
        "c": nrm(ks[1], (BATCH, D), 1.0),
        "ada_w": nrm(ks[2], (L, D, 6 * D), 0.5 * D ** -0.5),
        "ada_b": nrm(ks[3], (L, 6 * D), 0.02),
        "norm1_g": 1.0 + nrm(ks[4], (L, D), 0.02),
        "w_in": nrm(ks[5], (L, D, IN_WIDTH), D ** -0.5),
        "m_conv_w": nrm(ks[6], (L, M_CONV, 2 * M_QK_W), M_CONV ** -0.5),
        "m_conv_b": nrm(ks[7], (L, 2 * M_QK_W), 0.02),
        "m_i_bias": nrm(ks[8], (L, M_HEADS), 0.1),
        "m_f_bias": 3.0 + nrm(ks[9], (L, M_HEADS), 0.5),
        "m_out_norm_g": 1.0 + nrm(ks[10], (L, M_HEADS, M_DV), 0.02),
        "f_f_bias": 3.0 + nrm(ks[11], (L, F_HEADS), 0.5),
        "f_q_norm_g": 1.0 + nrm(ks[12], (L, F_HEADS, F_DH), 0.02),
        "f_k_norm_g": 1.0 + nrm(ks[13], (L, F_HEADS, F_DH), 0.02),
        "w_branch_m": nrm(ks[14], (L, M_V_W, D), M_V_W ** -0.5),
        "w_branch_f": nrm(ks[15], (L, F_W, D), F_W ** -0.5),
        "w_out": nrm(ks[16], (L, D, D), D ** -0.5),
        "norm2_g": 1.0 + nrm(ks[17], (L, D), 0.02),
        "w_router": nrm(ks[18], (L, D, N_EXPERTS), D ** -0.5),
        "b_router": nrm(ks[19], (L, N_EXPERTS), 0.01),
        "w_gate_up": nrm(ks[20], (L, N_EXPERTS, D, 2 * D_FF), D ** -0.5),
        "b_gate_up": nrm(ks[21], (L, N_EXPERTS, 2 * D_FF), 0.02),
        "w_down": nrm(ks[22], (L, N_EXPERTS, D_FF, D), D_FF ** -0.5),
        "b_down": nrm(ks[23], (L, N_EXPERTS, D), 0.02),
        "final_norm_g": 1.0 + nrm(ks[24], (D,), 0.02),
    }


def reference(x, c, ada_w, ada_b, norm1_g, w_in, m_conv_w, m_conv_b, m_i_bias, m_f_bias,
              m_out_norm_g, f_f_bias, f_q_norm_g, f_k_norm_g, w_branch_m, w_branch_f, w_out,
              norm2_g, w_router, b_router, w_gate_up, b_gate_up, w_down, b_down, final_norm_g):
    sc = jax.nn.silu(c)
    for l in range(DEPTH):
        mod = (sc @ ada_w[l] + ada_b[l])[:, None, :]
        sh1, sc1, g1, sh2, sc2, g2 = jnp.split(mod, 6, axis=-1)
        h = rmsnorm(x, norm1_g[l]) * (1.0 + sc1) + sh1
        x = x + g1 * hybrid_mixer(h, w_in[l], m_conv_w[l], m_conv_b[l], m_i_bias[l], m_f_bias[l],
                                  m_out_norm_g[l], f_f_bias[l], f_q_norm_g[l], f_k_norm_g[l],
                                  w_branch_m[l], w_branch_f[l], w_out[l])
        h = rmsnorm(x, norm2_g[l]) * (1.0 + sc2) + sh2
        x = x + g2 * moe_ffn(h, w_router[l], b_router[l], w_gate_up[l], b_gate_up[l],
                             w_down[l], b_down[l])
    return rmsnorm(x, final_norm_g)
```

```python
import functools

import jax
import jax.numpy as jnp
from jax import lax
from jax.experimental import pallas as pl
from jax.experimental.pallas import tpu as pltpu

F32 = jnp.float32
BF16 = jnp.bfloat16

D_MODEL = 2048
M_HEADS = 8
M_DQK = 64
M_DV = 128
M_CONV = 4
F_HEADS = 8
F_DH = 128
N_EXPERTS = 32
TOP_K = 4
D_FF = 2048
SWIGLU_LIMIT = 7.0
SWIGLU_ALPHA = 1.702
EPS = 1e-6

M_QK_W = M_HEADS * M_DQK
M_V_W = M_HEADS * M_DV
F_W = F_HEADS * F_DH

VMEM_LIMIT = 56 * 1024 * 1024
LANES = 128

PK_BLOCK = 1024
PK_NBLK = 11
BLK_GA, BLK_GB, BLK_MQK, BLK_MV, BLK_MO, BLK_FQ, BLK_FK, BLK_FV, BLK_FO = 0, 2, 4, 5, 6, 7, 8, 9, 10

IN_TM = 1024
MLSTM_L = 256
FOX_T = 512
OUT_TM = 512
MOE_TM = 256
MOE_TF = 512
MOE_TN = 1024
FIN_TM = 512


def _cparams(sem):
    return pltpu.CompilerParams(dimension_semantics=sem, vmem_limit_bytes=VMEM_LIMIT)


def _sigmoid(x):
    return 1.0 / (1.0 + jnp.exp(-x))


ADA_TN = 1024


def _ada_kernel(c_ref, w_ref, b_ref, o_ref):
    nrow = w_ref.shape[0] // 8

    def body(r, acc):
        rows = pl.ds(pl.multiple_of(r * 8, 8), 8)
        c = c_ref[rows, :]
        sc = c * _sigmoid(c)
        return acc + sc * w_ref[rows, :]

    acc = lax.fori_loop(0, nrow, body, jnp.zeros((8, w_ref.shape[1]), F32), unroll=8)
    o_ref[...] = jnp.sum(acc, axis=0, keepdims=True) + b_ref[...]


def _ada_mod(c_col, ada_w, ada_b):
    d, n = ada_w.shape
    return pl.pallas_call(
        _ada_kernel,
        out_shape=jax.ShapeDtypeStruct((1, n), F32),
        grid=(n // ADA_TN,),
        in_specs=[
            pl.BlockSpec((d, 1), lambda j: (0, 0)),
            pl.BlockSpec((d, ADA_TN), lambda j: (0, j)),
            pl.BlockSpec((1, ADA_TN), lambda j: (0, j)),
        ],
        out_specs=pl.BlockSpec((1, ADA_TN), lambda j: (0, j)),
        compiler_params=_cparams(("arbitrary",)),
        name="ada_mod",
    )(c_col, ada_w, ada_b)


def _head_rmsnorm(a, g_row, scale):
    outs = []
    for h in range(F_HEADS):
        ah = a[:, h * F_DH:(h + 1) * F_DH]
        ms = jnp.mean(ah * ah, axis=-1, keepdims=True)
        outs.append(ah * lax.rsqrt(ms + EPS) * (g_row[:, h * F_DH:(h + 1) * F_DH] * scale))
    return jnp.concatenate(outs, axis=-1)


def _inproj_kernel(x_ref, n1_ref, sc1_ref, sh1_ref, w_ref, wg_ref, qg_ref, kg_ref,
                   z_ref, zg_ref, h_s):
    n = pl.program_id(1)

    @pl.when(n == 0)
    def _():
        x = x_ref[...]
        y = x * lax.rsqrt(jnp.mean(x * x, axis=-1, keepdims=True) + EPS) * n1_ref[...]
        h = y * (1.0 + sc1_ref[...]) + sh1_ref[...]
        h_s[...] = h.astype(BF16)
        zg_ref[...] = jnp.dot(h, wg_ref[...], preferred_element_type=F32,
                              precision=lax.Precision.HIGHEST)

    acc = jnp.dot(h_s[...], w_ref[...], preferred_element_type=F32)

    @pl.when(n == BLK_FQ)
    def _():
        z_ref[...] = _head_rmsnorm(acc, qg_ref[...], F_DH ** -0.5).astype(BF16)

    @pl.when(n == BLK_FK)
    def _():
        z_ref[...] = _head_rmsnorm(acc, kg_ref[...], 1.0).astype(BF16)

    @pl.when(jnp.logical_and(n != BLK_FQ, n != BLK_FK))
    def _():
        z_ref[...] = acc.astype(BF16)


def _inproj(x, n1, sc1, sh1, w_pk, w_gates, qg, kg):
    s, d = x.shape
    row = lambda m, n: (0, 0)
    return pl.pallas_call(
        _inproj_kernel,
        out_shape=(jax.ShapeDtypeStruct((s, PK_NBLK * PK_BLOCK), BF16),
                   jax.ShapeDtypeStruct((s, LANES), F32)),
        grid=(s // IN_TM, PK_NBLK),
        in_specs=[
            pl.BlockSpec((IN_TM, d), lambda m, n: (m, 0)),
            pl.BlockSpec((1, d), row),
            pl.BlockSpec((1, d), row),
            pl.BlockSpec((1, d), row),
            pl.BlockSpec((d, PK_BLOCK), lambda m, n: (0, n)),
            pl.BlockSpec((d, LANES), row),
            pl.BlockSpec((1, PK_BLOCK), row),
            pl.BlockSpec((1, PK_BLOCK), row),
        ],
        out_specs=(pl.BlockSpec((IN_TM, PK_BLOCK), lambda m, n: (m, n)),
                   pl.BlockSpec((IN_TM, LANES), lambda m, n: (m, 0))),
        scratch_shapes=[pltpu.VMEM((IN_TM, d), BF16)],
        compiler_params=_cparams(("arbitrary", "arbitrary")),
        name="inproj",
    )(x, n1, sc1, sh1, w_pk, w_gates, qg, kg)


def _mlstm_kernel(qk_ref, v_ref, og_ref, bcol_ref, icol_ref, brow_ref, irow_ref,
                  cw_ref, cb_ref, gn_ref, out_ref, xbuf, c_s, m_s, bend_s):
    c = pl.program_id(0)
    L = qk_ref.shape[0]

    @pl.when(c == 0)
    def _():
        xbuf[0:8, :] = jnp.zeros((8, xbuf.shape[1]), F32)
        c_s[...] = jnp.zeros(c_s.shape, F32)
        m_s[...] = jnp.zeros(m_s.shape, F32)
        bend_s[...] = jnp.zeros(bend_s.shape, F32)

    xbuf[8:8 + L, :] = qk_ref[...].astype(F32)
    y = cb_ref[...] + cw_ref[0:1, :] * xbuf[5:5 + L, :]
    for j in range(1, M_CONV):
        y = y + cw_ref[j:j + 1, :] * xbuf[5 + j:5 + j + L, :]
    xbuf[0:8, :] = xbuf[L:L + 8, :]
    y = y * _sigmoid(y)

    rows = lax.broadcasted_iota(jnp.int32, (L, L), 0)
    cols = lax.broadcasted_iota(jnp.int32, (L, L), 1)
    causal = rows >= cols
    lane = lax.broadcasted_iota(jnp.int32, (L, M_DV), 1)
    ones_blk = jnp.where(lane == 0, 1.0, 0.0).astype(BF16)

    for h in range(M_HEADS):
        q = y[:, h * M_DQK:(h + 1) * M_DQK].astype(BF16)
        kf = y[:, M_QK_W + h * M_DQK:M_QK_W + (h + 1) * M_DQK] * (M_DQK ** -0.5)
        k = kf.astype(BF16)
        vext = jnp.concatenate([v_ref[:, h * M_DV:(h + 1) * M_DV], ones_blk], axis=-1)
        bcol = bcol_ref[:, h:h + 1]
        icol = icol_ref[:, h:h + 1]
        rowterm = irow_ref[h:h + 1, :] - brow_ref[h:h + 1, :]
        bprev = bend_s[h:h + 1, 0:1]
        mprev = m_s[h:h + 1, 0:1]

        dmat = jnp.where(causal, bcol + rowterm, -jnp.inf)
        inter = (bcol - bprev) + mprev
        m_t = jnp.maximum(inter, jnp.max(dmat, axis=1, keepdims=True))
        w_intra = jnp.exp(dmat - m_t)
        w_inter = jnp.exp(inter - m_t)
        s = lax.dot_general(q, k, (((1,), (1,)), ((), ())), preferred_element_type=F32) * w_intra
        cx = c_s[h]
        nd = (jnp.dot(s.astype(BF16), vext, preferred_element_type=F32)
              + w_inter * jnp.dot(q, cx.astype(BF16), preferred_element_type=F32))
        num = nd[:, :M_DV]
        den = nd[:, M_DV:M_DV + 1]
        hv = num / jnp.maximum(jnp.abs(den), jnp.exp(-m_t))
        ms = jnp.mean(hv * hv, axis=-1, keepdims=True)
        hn = hv * lax.rsqrt(ms + EPS) * gn_ref[:, h * M_DV:(h + 1) * M_DV]
        og = og_ref[:, h * M_DV:(h + 1) * M_DV].astype(F32)
        out_ref[:, h * M_DV:(h + 1) * M_DV] = (hn * _sigmoid(og)).astype(BF16)

        bend = bcol[L - 1:L, :]
        gtot = bend - bprev
        acol = (bend - bcol) + icol
        m_new = jnp.maximum(gtot + mprev, jnp.max(acol, axis=0, keepdims=True))
        wk = jnp.exp(acol - m_new)
        decay = jnp.exp(gtot + mprev - m_new)
        kw = (kf * wk).astype(BF16)
        upd = lax.dot_general(kw, vext, (((0,), (0,)), ((), ())), preferred_element_type=F32)
        c_s[h] = decay * cx + upd
        m_s[h:h + 1, :] = jnp.broadcast_to(m_new, (1, LANES))
        bend_s[h:h + 1, :] = jnp.broadcast_to(bend, (1, LANES))


def _mlstm(z, bcol, icol, brow, irow, conv_w, conv_b, gnorm):
    s = z.shape[0]
    L = MLSTM_L
    const = lambda c: (0, 0)
    return pl.pallas_call(
        _mlstm_kernel,
        out_shape=jax.ShapeDtypeStruct((s, M_V_W), BF16),
        grid=(s // L,),
        in_specs=[
            pl.BlockSpec((L, PK_BLOCK), lambda c: (c, BLK_MQK)),
            pl.BlockSpec((L, PK_BLOCK), lambda c: (c, BLK_MV)),
            pl.BlockSpec((L, PK_BLOCK), lambda c: (c, BLK_MO)),
            pl.BlockSpec((L, M_HEADS), lambda c: (c, 0)),
            pl.BlockSpec((L, M_HEADS), lambda c: (c, 0)),
            pl.BlockSpec((M_HEADS, L), lambda c: (0, c)),
            pl.BlockSpec((M_HEADS, L), lambda c: (0, c)),
            pl.BlockSpec((M_CONV, 2 * M_QK_W), const),
            pl.BlockSpec((1, 2 * M_QK_W), const),
            pl.BlockSpec((1, M_V_W), const),
        ],
        out_specs=pl.BlockSpec((L, M_V_W), lambda c: (c, 0)),
        scratch_shapes=[
            pltpu.VMEM((L + 8, 2 * M_QK_W), F32),
            pltpu.VMEM((M_HEADS, M_DQK, 2 * M_DV), F32),
            pltpu.VMEM((M_HEADS, LANES), F32),
            pltpu.VMEM((M_HEADS, LANES), F32),
        ],
        compiler_params=_cparams(("arbitrary",)),
        name="mlstm",
    )(z, z, z, bcol, icol, brow, irow, conv_w, conv_b, gnorm)


def _fox_kernel(q_ref, k_ref, v_ref, nf_ref, og_ref, o_ref, vext_s):
    i = pl.program_id(1)
    T = q_ref.shape[0]
    S = k_ref.shape[0]

    @pl.when(i == 0)
    def _():
        lane = lax.broadcasted_iota(jnp.int32, (S, F_DH), 1)
        vext_s[:, 0:F_DH] = v_ref[...]
        vext_s[:, F_DH:2 * F_DH] = jnp.where(lane == 0, 1.0, 0.0).astype(BF16)

    q = q_ref[...]

    def scores(j):
        off = pl.multiple_of(j * T, T)
        ks = k_ref[pl.ds(off, T), :]
        s = lax.dot_general(q, ks, (((1,), (1,)), ((), ())), preferred_element_type=F32)
        return s + nf_ref[0, :, pl.ds(off, T)], off

    def update(s, off, carry):
        m, acc = carry
        m_new = jnp.maximum(m, jnp.max(s, axis=1, keepdims=True))
        p = jnp.exp(s - m_new)
        alpha = jnp.exp(m - m_new)
        acc = alpha * acc + jnp.dot(p.astype(BF16), vext_s[pl.ds(off, T), :],
                                    preferred_element_type=F32)
        return m_new, acc

    def body(j, carry):
        s, off = scores(j)
        return update(s, off, carry)

    init = (jnp.full((T, 1), -jnp.inf, F32), jnp.zeros((T, 2 * F_DH), F32))
    carry = lax.fori_loop(0, i, body, init)
    s, off = scores(i)
    rows = lax.broadcasted_iota(jnp.int32, (T, T), 0)
    cols = lax.broadcasted_iota(jnp.int32, (T, T), 1)
    s = jnp.where(rows >= cols, s, -jnp.inf)
    _, acc = update(s, off, carry)
    out = acc[:, :F_DH] / acc[:, F_DH:F_DH + 1]
    o_ref[...] = (out * _sigmoid(og_ref[...].astype(F32))).astype(BF16)


def _fox(z, negf):
    s = z.shape[0]
    T = FOX_T
    cpb = PK_BLOCK // F_DH
    return pl.pallas_call(
        _fox_kernel,
        out_shape=jax.ShapeDtypeStruct((s, F_W), BF16),
        grid=(F_HEADS, s // T),
        in_specs=[
            pl.BlockSpec((T, F_DH), lambda h, i: (i, BLK_FQ * cpb + h)),
            pl.BlockSpec((s, F_DH), lambda h, i: (0, BLK_FK * cpb + h)),
            pl.BlockSpec((s, F_DH), lambda h, i: (0, BLK_FV * cpb + h)),
            pl.BlockSpec((1, 1, s), lambda h, i: (h, 0, 0)),
            pl.BlockSpec((T, F_DH), lambda h, i: (i, BLK_FO * cpb + h)),
        ],
        out_specs=pl.BlockSpec((T, F_DH), lambda h, i: (i, h)),
        scratch_shapes=[pltpu.VMEM((s, 2 * F_DH), BF16)],
        compiler_params=_cparams(("arbitrary", "arbitrary")),
        name="fox_attention",
    )(z, z, z, negf, z)


def _outproj_kernel(hm_ref, hf_ref, ga_ref, gb_ref, x_ref, wm_ref, wf_ref, wo_ref,
                    g1_ref, n2_ref, sc2_ref, sh2_ref, wr_ref, br_ref,
                    x1_ref, h2_ref, lg_ref):
    ym = jnp.dot(hm_ref[...], wm_ref[...], preferred_element_type=F32)
    yf = jnp.dot(hf_ref[...], wf_ref[...], preferred_element_type=F32)
    y = (_sigmoid(ga_ref[...].astype(F32)) * ym + _sigmoid(gb_ref[...].astype(F32)) * yf)
    mix = jnp.dot(y.astype(BF16), wo_ref[...], preferred_element_type=F32)
    x1 = x_ref[...] + g1_ref[...] * mix
    x1_ref[...] = x1
    hn = x1 * lax.rsqrt(jnp.mean(x1 * x1, axis=-1, keepdims=True) + EPS) * n2_ref[...]
    h2 = hn * (1.0 + sc2_ref[...]) + sh2_ref[...]
    h2_ref[...] = h2.astype(BF16)
    lg_ref[...] = jnp.dot(h2, wr_ref[...], preferred_element_type=F32,
                          precision=lax.Precision.HIGHEST) + br_ref[...]


def _outproj(hm, hf, z, x, wm, wf, wo, g1, n2, sc2, sh2, wr, br):
    s, d = x.shape
    tm = OUT_TM
    const = lambda m: (0, 0)
    gw = 2 * PK_BLOCK
    return pl.pallas_call(
        _outproj_kernel,
        out_shape=(jax.ShapeDtypeStruct((s, d), F32),
                   jax.ShapeDtypeStruct((s, d), BF16),
                   jax.ShapeDtypeStruct((s, LANES), F32)),
        grid=(s // tm,),
        in_specs=[
            pl.BlockSpec((tm, M_V_W), lambda m: (m, 0)),
            pl.BlockSpec((tm, F_W), lambda m: (m, 0)),
            pl.BlockSpec((tm, gw), lambda m: (m, BLK_GA // 2)),
            pl.BlockSpec((tm, gw), lambda m: (m, BLK_GB // 2)),
            pl.BlockSpec((tm, d), lambda m: (m, 0)),
            pl.BlockSpec((M_V_W, d), const),
            pl.BlockSpec((F_W, d), const),
            pl.BlockSpec((d, d), const),
            pl.BlockSpec((1, d), const),
            pl.BlockSpec((1, d), const),
            pl.BlockSpec((1, d), const),
            pl.BlockSpec((1, d), const),
            pl.BlockSpec((d, LANES), const),
            pl.BlockSpec((1, LANES), const),
        ],
        out_specs=(pl.BlockSpec((tm, d), lambda m: (m, 0)),
                   pl.BlockSpec((tm, d), lambda m: (m, 0)),
                   pl.BlockSpec((tm, LANES), lambda m: (m, 0))),
        compiler_params=_cparams(("arbitrary",)),
        name="outproj",
    )(hm, hf, z, z, x, wm, wf, wo, g1, n2, sc2, sh2, wr, br)


def _moe_gu_kernel(te_ref, nu_ref, x_ref, wg_ref, wu_ref, bg_ref, bu_ref, o_ref):
    i = pl.program_id(1)

    @pl.when(i < nu_ref[0])
    def _():
        x = x_ref[...]
        g = jnp.dot(x, wg_ref[0].astype(BF16), preferred_element_type=F32) + bg_ref[0]
        u = jnp.dot(x, wu_ref[0].astype(BF16), preferred_element_type=F32) + bu_ref[0]
        gate = jnp.minimum(g, SWIGLU_LIMIT)
        up = jnp.clip(u, -SWIGLU_LIMIT, SWIGLU_LIMIT)
        act = (up + 1.0) * gate * _sigmoid(SWIGLU_ALPHA * gate)
        o_ref[...] = act.astype(BF16)

    @pl.when(i >= nu_ref[0])
    def _():
        o_ref[...] = jnp.zeros(o_ref.shape, BF16)


def _moe_gate_up(tile_e, n_used, xs, w_gu, b_gu):
    r, d = xs.shape
    nf = D_FF // MOE_TF
    grid_spec = pltpu.PrefetchScalarGridSpec(
        num_scalar_prefetch=2,
        grid=(nf, r // MOE_TM),
        in_specs=[
            pl.BlockSpec((MOE_TM, d), lambda f, i, te, nu: (i, 0)),
            pl.BlockSpec((1, d, MOE_TF), lambda f, i, te, nu: (te[i], 0, f)),
            pl.BlockSpec((1, d, MOE_TF), lambda f, i, te, nu: (te[i], 0, nf + f)),
            pl.BlockSpec((1, 1, MOE_TF), lambda f, i, te, nu: (te[i], 0, f)),
            pl.BlockSpec((1, 1, MOE_TF), lambda f, i, te, nu: (te[i], 0, nf + f)),
        ],
        out_specs=pl.BlockSpec((MOE_TM, MOE_TF), lambda f, i, te, nu: (i, f)),
    )
    return pl.pallas_call(
        _moe_gu_kernel,
        out_shape=jax.ShapeDtypeStruct((r, D_FF), BF16),
        grid_spec=grid_spec,
        compiler_params=_cparams(("arbitrary", "arbitrary")),
        name="moe_gate_up",
    )(tile_e, n_used, xs, w_gu, w_gu, b_gu, b_gu)


def _moe_down_kernel(te_ref, nu_ref, a_ref, wd_ref, bd_ref, rw_ref, o_ref):
    i = pl.program_id(1)

    @pl.when(i < nu_ref[0])
    def _():
        y = jnp.dot(a_ref[...], wd_ref[0].astype(BF16), preferred_element_type=F32) + bd_ref[0]
        o_ref[...] = (y * rw_ref[...]).astype(BF16)

    @pl.when(i >= nu_ref[0])
    def _():
        o_ref[...] = jnp.zeros(o_ref.shape, BF16)


def _moe_down(tile_e, n_used, act, w_d, b_d, row_w):
    r = act.shape[0]
    nn = D_MODEL // MOE_TN
    grid_spec = pltpu.PrefetchScalarGridSpec(
        num_scalar_prefetch=2,
        grid=(nn, r // MOE_TM),
        in_specs=[
            pl.BlockSpec((MOE_TM, D_FF), lambda n, i, te, nu: (i, 0)),
            pl.BlockSpec((1, D_FF, MOE_TN), lambda n, i, te, nu: (te[i], 0, n)),
            pl.BlockSpec((1, 1, MOE_TN), lambda n, i, te, nu: (te[i], 0, n)),
            pl.BlockSpec((MOE_TM, 1), lambda n, i, te, nu: (i, 0)),
        ],
        out_specs=pl.BlockSpec((MOE_TM, MOE_TN), lambda n, i, te, nu: (i, n)),
    )
    return pl.pallas_call(
        _moe_down_kernel,
        out_shape=jax.ShapeDtypeStruct((r, D_MODEL), BF16),
        grid_spec=grid_spec,
        compiler_params=_cparams(("arbitrary", "arbitrary")),
        name="moe_down",
    )(tile_e, n_used, act, w_d, b_d, row_w)


def _final_kernel(x1_ref, yg_ref, g2_ref, gf_ref, o_ref):
    moe = yg_ref[0].astype(F32)
    for k in range(1, TOP_K):
        moe = moe + yg_ref[k].astype(F32)
    x2 = x1_ref[...] + g2_ref[...] * moe
    o_ref[...] = x2 * lax.rsqrt(jnp.mean(x2 * x2, axis=-1, keepdims=True) + EPS) * gf_ref[...]


def _final(x1, yg, g2, gf):
    s, d = x1.shape
    tm = FIN_TM
    const = lambda m: (0, 0)
    return pl.pallas_call(
        _final_kernel,
        out_shape=jax.ShapeDtypeStruct((s, d), F32),
        grid=(s // tm,),
        in_specs=[
            pl.BlockSpec((tm, d), lambda m: (m, 0)),
            pl.BlockSpec((TOP_K, tm, d), lambda m: (0, m, 0)),
            pl.BlockSpec((1, d), const),
            pl.BlockSpec((1, d), const),
        ],
        out_specs=pl.BlockSpec((tm, d), lambda m: (m, 0)),
        compiler_params=_cparams(("arbitrary",)),
        name="final_combine",
    )(x1, yg, g2, gf)


def _routing(logits, n_rows):
    t = logits.shape[0]
    top_val, top_idx = lax.top_k(logits, TOP_K)
    top_w = jax.nn.softmax(top_val, axis=-1)
    flat_e = top_idx.reshape(-1).astype(jnp.int32)
    flat_w = top_w.reshape(-1)
    onehot = (flat_e[:, None] == jnp.arange(N_EXPERTS, dtype=jnp.int32)[None, :]).astype(jnp.int32)
    csum = jnp.cumsum(onehot, axis=0)
    rank = jnp.take_along_axis(csum, flat_e[:, None], axis=1)[:, 0] - 1
    counts = csum[-1]
    padded = ((counts + MOE_TM - 1) // MOE_TM) * MOE_TM
    pend = jnp.cumsum(padded)
    pstart = pend - padded
    pos = pstart[flat_e] + rank
    tok = jnp.arange(t * TOP_K, dtype=jnp.int32) // TOP_K
    row_tok = jnp.zeros((n_rows,), jnp.int32).at[pos].set(tok)
    row_w = jnp.zeros((n_rows,), F32).at[pos].set(flat_w)
    n_used = (pend[-1] // MOE_TM).astype(jnp.int32)
    tile_start = jnp.arange(n_rows // MOE_TM, dtype=jnp.int32) * MOE_TM
    tile_e = jnp.searchsorted(pend, tile_start, side="right").astype(jnp.int32)
    last_e = jnp.searchsorted(pend, pend[-1] - 1, side="right").astype(jnp.int32)
    tile_e = jnp.minimum(tile_e, last_e)
    return pos.reshape(t, TOP_K), row_tok, row_w, tile_e, n_used.reshape(1)


def _pack_w_in(w_in):
    sizes = (M_QK_W, M_QK_W, M_V_W, M_HEADS, M_HEADS, M_V_W, F_W, F_W, F_W, F_HEADS, F_W,
             D_MODEL, D_MODEL)
    offs = [0]
    for sz in sizes:
        offs.append(offs[-1] + sz)
    seg = lambda i: w_in[:, offs[i]:offs[i + 1]]
    packed = jnp.concatenate([seg(11), seg(12), seg(0), seg(1), seg(2), seg(5), seg(6), seg(7),
                              seg(8), seg(10)], axis=1).astype(BF16)
    gates = jnp.concatenate([seg(3), seg(4), seg(9),
                             jnp.zeros((w_in.shape[0], LANES - 3 * M_HEADS), F32)], axis=1)
    return packed, gates


def kernel(x, c, ada_w, ada_b, norm1_g, w_in, m_conv_w, m_conv_b, m_i_bias, m_f_bias,
           m_out_norm_g, f_f_bias, f_q_norm_g, f_k_norm_g, w_branch_m, w_branch_f, w_out,
           norm2_g, w_router, b_router, w_gate_up, b_gate_up, w_down, b_down, final_norm_g):
    b, s, d = x.shape
    assert b == 1 and ada_w.shape[0] == 1
    x2d = x.reshape(s, d)

    mod = _ada_mod(c.reshape(d, 1), ada_w.reshape(d, 6 * d), ada_b.reshape(1, 6 * d))
    sh1, sc1, g1, sh2, sc2, g2 = [mod[:, i * d:(i + 1) * d] for i in range(6)]

    w_pk, w_gates = _pack_w_in(w_in.reshape(d, -1))
    z, zg = _inproj(x2d, norm1_g.reshape(1, d), sc1, sh1, w_pk, w_gates,
                    f_q_norm_g.reshape(1, F_W), f_k_norm_g.reshape(1, F_W))

    i_pre = zg[:, 0:M_HEADS] + m_i_bias.reshape(1, M_HEADS)
    logf_m = jax.nn.log_sigmoid(zg[:, M_HEADS:2 * M_HEADS] + m_f_bias.reshape(1, M_HEADS))
    logf_f = jax.nn.log_sigmoid(zg[:, 2 * M_HEADS:3 * M_HEADS] + f_f_bias.reshape(1, F_HEADS))
    bm = jnp.cumsum(logf_m, axis=0)
    ff = jnp.cumsum(logf_f, axis=0)

    hm = _mlstm(z, bm, i_pre, bm.T, i_pre.T, m_conv_w.reshape(M_CONV, 2 * M_QK_W),
                m_conv_b.reshape(1, 2 * M_QK_W), m_out_norm_g.reshape(1, M_V_W))
    hf = _fox(z, (-ff.T).reshape(F_HEADS, 1, s))

    wr = jnp.concatenate([w_router.reshape(d, N_EXPERTS),
                          jnp.zeros((d, LANES - N_EXPERTS), F32)], axis=1)
    br = jnp.concatenate([b_router.reshape(1, N_EXPERTS),
                          jnp.zeros((1, LANES - N_EXPERTS), F32)], axis=1)
    x1, h2, lg = _outproj(hm, hf, z, x2d,
                          w_branch_m.reshape(M_V_W, d).astype(BF16),
                          w_branch_f.reshape(F_W, d).astype(BF16),
                          w_out.reshape(d, d).astype(BF16),
                          g1, norm2_g.reshape(1, d), sc2, sh2, wr, br)

    n_rows = s * TOP_K + N_EXPERTS * MOE_TM
    pos, row_tok, row_w, tile_e, n_used = _routing(lg[:, :N_EXPERTS], n_rows)
    xs = jnp.take(h2, row_tok, axis=0)
    act = _moe_gate_up(tile_e, n_used, xs, w_gate_up.reshape(N_EXPERTS, d, 2 * D_FF),
                       b_gate_up.reshape(N_EXPERTS, 1, 2 * D_FF))
    ys = _moe_down(tile_e, n_used, act, w_down.reshape(N_EXPERTS, D_FF, d),
                   b_down.reshape(N_EXPERTS, 1, d), row_w.reshape(n_rows, 1))
    yg = jnp.take(ys, pos.T, axis=0)
    out = _final(x1, yg, g2, final_norm_g.reshape(1, d))
    return out.reshape(b, s, d)
```

```python
import functools

import jax
import jax.numpy as jnp
from jax import lax
from jax.experimental import pallas as pl
from jax.experimental.pallas import tpu as pltpu

F32 = jnp.float32
BF16 = jnp.bfloat16

D_MODEL = 2048
M_HEADS = 8
M_DQK = 64
M_DV = 128
M_CONV = 4
F_HEADS = 8
F_DH = 128
N_EXPERTS = 32
TOP_K = 4
D_FF = 2048
SWIGLU_LIMIT = 7.0
SWIGLU_ALPHA = 1.702
EPS = 1e-6

M_QK_W = M_HEADS * M_DQK
M_V_W = M_HEADS * M_DV
F_W = F_HEADS * F_DH

VMEM_LIMIT = 56 * 1024 * 1024
LANES = 128

PK_BLOCK = 1024
PK_NBLK = 11
BLK_GA, BLK_GB, BLK_MQK, BLK_MV, BLK_MO, BLK_FQ, BLK_FK, BLK_FV, BLK_FO = 0, 2, 4, 5, 6, 7, 8, 9, 10

IN_TM = 1024
MLSTM_L = 256
FOX_T = 512
OUT_TM = 256
MOE_TM = 256
MOE_TF = 1024
MOE_TN = 1024
FIN_TM = 256


def _cparams(sem):
    return pltpu.CompilerParams(dimension_semantics=sem, vmem_limit_bytes=VMEM_LIMIT)


def _sigmoid(x):
    return 1.0 / (1.0 + jnp.exp(-x))


ADA_TN = 1024


def _ada_kernel(c_ref, w_ref, b_ref, o_ref):
    nrow = w_ref.shape[0] // 8

    def body(r, acc):
        rows = pl.ds(pl.multiple_of(r * 8, 8), 8)
        c = c_ref[rows, :]
        sc = c * _sigmoid(c)
        return acc + sc * w_ref[rows, :]

    acc = lax.fori_loop(0, nrow, body, jnp.zeros((8, w_ref.shape[1]), F32), unroll=8)
    o_ref[...] = jnp.sum(acc, axis=0, keepdims=True) + b_ref[...]


def _ada_mod(c_col, ada_w, ada_b):
    d, n = ada_w.shape
    return pl.pallas_call(
        _ada_kernel,
        out_shape=jax.ShapeDtypeStruct((1, n), F32),
        grid=(n // ADA_TN,),
        in_specs=[
            pl.BlockSpec((d, 1), lambda j: (0, 0)),
            pl.BlockSpec((d, ADA_TN), lambda j: (0, j)),
            pl.BlockSpec((1, ADA_TN), lambda j: (0, j)),
        ],
        out_specs=pl.BlockSpec((1, ADA_TN), lambda j: (0, j)),
        compiler_params=_cparams(("arbitrary",)),
        name="ada_mod",
    )(c_col, ada_w, ada_b)


def _head_rmsnorm(a, g_row, scale):
    outs = []
    for h in range(F_HEADS):
        ah = a[:, h * F_DH:(h + 1) * F_DH]
        ms = jnp.mean(ah * ah, axis=-1, keepdims=True)
        outs.append(ah * lax.rsqrt(ms + EPS) * (g_row[:, h * F_DH:(h + 1) * F_DH] * scale))
    return jnp.concatenate(outs, axis=-1)


def _inproj_kernel(x_ref, n1_ref, sc1_ref, sh1_ref, w_ref, wg_ref, qg_ref, kg_ref,
                   z_ref, zg_ref, h_s):
    n = pl.program_id(1)

    @pl.when(n == 0)
    def _():
        x = x_ref[...]
        y = x * lax.rsqrt(jnp.mean(x * x, axis=-1, keepdims=True) + EPS) * n1_ref[...]
        h = y * (1.0 + sc1_ref[...]) + sh1_ref[...]
        h_s[...] = h.astype(BF16)
        zg_ref[...] = jnp.dot(h, wg_ref[...], preferred_element_type=F32,
                              precision=lax.Precision.HIGHEST)

    acc = jnp.dot(h_s[...], w_ref[...], preferred_element_type=F32)

    @pl.when(n == BLK_FQ)
    def _():
        z_ref[...] = _head_rmsnorm(acc, qg_ref[...], F_DH ** -0.5).astype(BF16)

    @pl.when(n == BLK_FK)
    def _():
        z_ref[...] = _head_rmsnorm(acc, kg_ref[...], 1.0).astype(BF16)

    @pl.when(jnp.logical_and(n != BLK_FQ, n != BLK_FK))
    def _():
        z_ref[...] = acc.astype(BF16)


def _inproj(x, n1, sc1, sh1, w_pk, w_gates, qg, kg):
    s, d = x.shape
    row = lambda m, n: (0, 0)
    return pl.pallas_call(
        _inproj_kernel,
        out_shape=(jax.ShapeDtypeStruct((s, PK_NBLK * PK_BLOCK), BF16),
                   jax.ShapeDtypeStruct((s, LANES), F32)),
        grid=(s // IN_TM, PK_NBLK),
        in_specs=[
            pl.BlockSpec((IN_TM, d), lambda m, n: (m, 0)),
            pl.BlockSpec((1, d), row),
            pl.BlockSpec((1, d), row),
            pl.BlockSpec((1, d), row),
            pl.BlockSpec((d, PK_BLOCK), lambda m, n: (0, n)),
            pl.BlockSpec((d, LANES), row),
            pl.BlockSpec((1, PK_BLOCK), row),
            pl.BlockSpec((1, PK_BLOCK), row),
        ],
        out_specs=(pl.BlockSpec((IN_TM, PK_BLOCK), lambda m, n: (m, n)),
                   pl.BlockSpec((IN_TM, LANES), lambda m, n: (m, 0))),
        scratch_shapes=[pltpu.VMEM((IN_TM, d), BF16)],
        compiler_params=_cparams(("arbitrary", "arbitrary")),
        name="inproj",
    )(x, n1, sc1, sh1, w_pk, w_gates, qg, kg)


def _mlstm_kernel(qk_ref, v_ref, og_ref, bcol_ref, icol_ref, brow_ref, irow_ref,
                  cw_ref, cb_ref, gn_ref, out_ref, xbuf, c_s, m_s, bend_s):
    c = pl.program_id(0)
    L = qk_ref.shape[0]

    @pl.when(c == 0)
    def _():
        xbuf[0:8, :] = jnp.zeros((8, xbuf.shape[1]), F32)
        c_s[...] = jnp.zeros(c_s.shape, F32)
        m_s[...] = jnp.zeros(m_s.shape, F32)
        bend_s[...] = jnp.zeros(bend_s.shape, F32)

    xbuf[8:8 + L, :] = qk_ref[...].astype(F32)
    y = cb_ref[...] + cw_ref[0:1, :] * xbuf[5:5 + L, :]
    for j in range(1, M_CONV):
        y = y + cw_ref[j:j + 1, :] * xbuf[5 + j:5 + j + L, :]
    xbuf[0:8, :] = xbuf[L:L + 8, :]
    y = y * _sigmoid(y)

    rows = lax.broadcasted_iota(jnp.int32, (L, L), 0)
    cols = lax.broadcasted_iota(jnp.int32, (L, L), 1)
    causal = rows >= cols
    lane = lax.broadcasted_iota(jnp.int32, (L, M_DV), 1)
    ones_blk = jnp.where(lane == 0, 1.0, 0.0).astype(BF16)

    for h in range(M_HEADS):
        q = y[:, h * M_DQK:(h + 1) * M_DQK].astype(BF16)
        kf = y[:, M_QK_W + h * M_DQK:M_QK_W + (h + 1) * M_DQK] * (M_DQK ** -0.5)
        k = kf.astype(BF16)
        vext = jnp.concatenate([v_ref[:, h * M_DV:(h + 1) * M_DV], ones_blk], axis=-1)
        bcol = bcol_ref[:, h:h + 1]
        icol = icol_ref[:, h:h + 1]
        rowterm = irow_ref[h:h + 1, :] - brow_ref[h:h + 1, :]
        bprev = bend_s[h:h + 1, 0:1]
        mprev = m_s[h:h + 1, 0:1]

        dmat = jnp.where(causal, bcol + rowterm, -jnp.inf)
        inter = (bcol - bprev) + mprev
        m_t = jnp.maximum(inter, jnp.max(dmat, axis=1, keepdims=True))
        w_intra = jnp.exp(dmat - m_t)
        w_inter = jnp.exp(inter - m_t)
        s = lax.dot_general(q, k, (((1,), (1,)), ((), ())), preferred_element_type=F32) * w_intra
        cx = c_s[h]
        nd = (jnp.dot(s.astype(BF16), vext, preferred_element_type=F32)
              + w_inter * jnp.dot(q, cx.astype(BF16), preferred_element_type=F32))
        num = nd[:, :M_DV]
        den = nd[:, M_DV:M_DV + 1]
        hv = num / jnp.maximum(jnp.abs(den), jnp.exp(-m_t))
        ms = jnp.mean(hv * hv, axis=-1, keepdims=True)
        hn = hv * lax.rsqrt(ms + EPS) * gn_ref[:, h * M_DV:(h + 1) * M_DV]
        og = og_ref[:, h * M_DV:(h + 1) * M_DV].astype(F32)
        out_ref[:, h * M_DV:(h + 1) * M_DV] = (hn * _sigmoid(og)).astype(BF16)

        bend = bcol[L - 1:L, :]
        gtot = bend - bprev
        acol = (bend - bcol) + icol
        m_new = jnp.maximum(gtot + mprev, jnp.max(acol, axis=0, keepdims=True))
        wk = jnp.exp(acol - m_new)
        decay = jnp.exp(gtot + mprev - m_new)
        kw = (kf * wk).astype(BF16)
        upd = lax.dot_general(kw, vext, (((0,), (0,)), ((), ())), preferred_element_type=F32)
        c_s[h] = decay * cx + upd
        m_s[h:h + 1, :] = jnp.broadcast_to(m_new, (1, LANES))
        bend_s[h:h + 1, :] = jnp.broadcast_to(bend, (1, LANES))


def _mlstm(z, bcol, icol, brow, irow, conv_w, conv_b, gnorm):
    s = z.shape[0]
    L = MLSTM_L
    const = lambda c: (0, 0)
    return pl.pallas_call(
        _mlstm_kernel,
        out_shape=jax.ShapeDtypeStruct((s, M_V_W), BF16),
        grid=(s // L,),
        in_specs=[
            pl.BlockSpec((L, PK_BLOCK), lambda c: (c, BLK_MQK)),
            pl.BlockSpec((L, PK_BLOCK), lambda c: (c, BLK_MV)),
            pl.BlockSpec((L, PK_BLOCK), lambda c: (c, BLK_MO)),
            pl.BlockSpec((L, M_HEADS), lambda c: (c, 0)),
            pl.BlockSpec((L, M_HEADS), lambda c: (c, 0)),
            pl.BlockSpec((M_HEADS, L), lambda c: (0, c)),
            pl.BlockSpec((M_HEADS, L), lambda c: (0, c)),
            pl.BlockSpec((M_CONV, 2 * M_QK_W), const),
            pl.BlockSpec((1, 2 * M_QK_W), const),
            pl.BlockSpec((1, M_V_W), const),
        ],
        out_specs=pl.BlockSpec((L, M_V_W), lambda c: (c, 0)),
        scratch_shapes=[
            pltpu.VMEM((L + 8, 2 * M_QK_W), F32),
            pltpu.VMEM((M_HEADS, M_DQK, 2 * M_DV), F32),
            pltpu.VMEM((M_HEADS, LANES), F32),
            pltpu.VMEM((M_HEADS, LANES), F32),
        ],
        compiler_params=_cparams(("arbitrary",)),
        name="mlstm",
    )(z, z, z, bcol, icol, brow, irow, conv_w, conv_b, gnorm)


def _fox_kernel(q_ref, k_ref, v_ref, nf_ref, og_ref, o_ref, vext_s):
    i = pl.program_id(1)
    T = q_ref.shape[0]
    S = k_ref.shape[0]

    @pl.when(i == 0)
    def _():
        lane = lax.broadcasted_iota(jnp.int32, (S, F_DH), 1)
        vext_s[:, 0:F_DH] = v_ref[...]
        vext_s[:, F_DH:2 * F_DH] = jnp.where(lane == 0, 1.0, 0.0).astype(BF16)

    H = T // 2
    qs = (q_ref[0:H, :], q_ref[H:T, :])

    def update(q, ks, vs, nf, carry, mask):
        m, acc = carry
        s = lax.dot_general(q, ks, (((1,), (1,)), ((), ())), preferred_element_type=F32) + nf
        if mask is not None:
            s = jnp.where(mask, s, -jnp.inf)
        m_new = jnp.maximum(m, jnp.max(s, axis=1, keepdims=True))
        p = jnp.exp(s - m_new)
        alpha = jnp.exp(m - m_new)
        acc = alpha * acc + jnp.dot(p.astype(BF16), vs, preferred_element_type=F32)
        return m_new, acc

    def block(j, carries, masks):
        off = pl.multiple_of(j * T, T)
        ks = k_ref[pl.ds(off, T), :]
        vs = vext_s[pl.ds(off, T), :]
        nf = nf_ref[0, :, pl.ds(off, T)]
        return tuple(update(q, ks, vs, nf, c, mk) for q, c, mk in zip(qs, carries, masks))

    init = tuple((jnp.full((H, 1), -jnp.inf, F32), jnp.zeros((H, 2 * F_DH), F32)) for _ in qs)
    carries = lax.fori_loop(0, i, lambda j, c: block(j, c, (None, None)), init)
    rows = lax.broadcasted_iota(jnp.int32, (H, T), 0)
    cols = lax.broadcasted_iota(jnp.int32, (H, T), 1)
    carries = block(i, carries, (rows >= cols, rows + H >= cols))
    for half, (_, acc) in enumerate(carries):
        out = acc[:, :F_DH] / acc[:, F_DH:F_DH + 1]
        og = og_ref[half * H:(half + 1) * H, :].astype(F32)
        o_ref[half * H:(half + 1) * H, :] = (out * _sigmoid(og)).astype(BF16)


def _fox(z, negf):
    s = z.shape[0]
    T = FOX_T
    cpb = PK_BLOCK // F_DH
    return pl.pallas_call(
        _fox_kernel,
        out_shape=jax.ShapeDtypeStruct((s, F_W), BF16),
        grid=(F_HEADS, s // T),
        in_specs=[
            pl.BlockSpec((T, F_DH), lambda h, i: (i, BLK_FQ * cpb + h)),
            pl.BlockSpec((s, F_DH), lambda h, i: (0, BLK_FK * cpb + h)),
            pl.BlockSpec((s, F_DH), lambda h, i: (0, BLK_FV * cpb + h)),
            pl.BlockSpec((1, 1, s), lambda h, i: (h, 0, 0)),
            pl.BlockSpec((T, F_DH), lambda h, i: (i, BLK_FO * cpb + h)),
        ],
        out_specs=pl.BlockSpec((T, F_DH), lambda h, i: (i, h)),
        scratch_shapes=[pltpu.VMEM((s, 2 * F_DH), BF16)],
        compiler_params=_cparams(("arbitrary", "arbitrary")),
        name="fox_attention",
    )(z, z, z, negf, z)


def _outproj_kernel(hm_ref, hf_ref, ga_ref, gb_ref, x_ref, wm_ref, wf_ref, wo_ref,
                    g1_ref, n2_ref, sc2_ref, sh2_ref, wr_ref, br_ref,
                    x1_ref, h2_ref, lg_ref):
    ym = jnp.dot(hm_ref[...], wm_ref[...], preferred_element_type=F32)
    yf = jnp.dot(hf_ref[...], wf_ref[...], preferred_element_type=F32)
    y = (_sigmoid(ga_ref[...].astype(F32)) * ym + _sigmoid(gb_ref[...].astype(F32)) * yf)
    mix = jnp.dot(y.astype(BF16), wo_ref[...], preferred_element_type=F32)
    x1 = x_ref[...] + g1_ref[...] * mix
    x1_ref[...] = x1
    hn = x1 * lax.rsqrt(jnp.mean(x1 * x1, axis=-1, keepdims=True) + EPS) * n2_ref[...]
    h2 = hn * (1.0 + sc2_ref[...]) + sh2_ref[...]
    h2_ref[...] = h2
    lg_ref[...] = jnp.dot(h2, wr_ref[...], preferred_element_type=F32,
                          precision=lax.Precision.HIGHEST) + br_ref[...]


def _outproj(hm, hf, z, x, wm, wf, wo, g1, n2, sc2, sh2, wr, br):
    s, d = x.shape
    tm = OUT_TM
    const = lambda m: (0, 0)
    gw = 2 * PK_BLOCK
    return pl.pallas_call(
        _outproj_kernel,
        out_shape=(jax.ShapeDtypeStruct((s, d), F32),
                   jax.ShapeDtypeStruct((s, d), F32),
                   jax.ShapeDtypeStruct((s, LANES), F32)),
        grid=(s // tm,),
        in_specs=[
            pl.BlockSpec((tm, M_V_W), lambda m: (m, 0)),
            pl.BlockSpec((tm, F_W), lambda m: (m, 0)),
            pl.BlockSpec((tm, gw), lambda m: (m, BLK_GA // 2)),
            pl.BlockSpec((tm, gw), lambda m: (m, BLK_GB // 2)),
            pl.BlockSpec((tm, d), lambda m: (m, 0)),
            pl.BlockSpec((M_V_W, d), const, pipeline_mode=pl.Buffered(1)),
            pl.BlockSpec((F_W, d), const, pipeline_mode=pl.Buffered(1)),
            pl.BlockSpec((d, d), const, pipeline_mode=pl.Buffered(1)),
            pl.BlockSpec((1, d), const),
            pl.BlockSpec((1, d), const),
            pl.BlockSpec((1, d), const),
            pl.BlockSpec((1, d), const),
            pl.BlockSpec((d, LANES), const),
            pl.BlockSpec((1, LANES), const),
        ],
        out_specs=(pl.BlockSpec((tm, d), lambda m: (m, 0)),
                   pl.BlockSpec((tm, d), lambda m: (m, 0)),
                   pl.BlockSpec((tm, LANES), lambda m: (m, 0))),
        compiler_params=_cparams(("arbitrary",)),
        name="outproj",
    )(hm, hf, z, z, x, wm, wf, wo, g1, n2, sc2, sh2, wr, br)


def _moe_gu_kernel(te_ref, nu_ref, x_ref, wg_ref, wu_ref, bg_ref, bu_ref, o_ref):
    i = pl.program_id(1)

    @pl.when(i < nu_ref[0])
    def _():
        x = x_ref[...].astype(BF16)
        g = jnp.dot(x, wg_ref[0].astype(BF16), preferred_element_type=F32) + bg_ref[0]
        u = jnp.dot(x, wu_ref[0].astype(BF16), preferred_element_type=F32) + bu_ref[0]
        gate = jnp.minimum(g, SWIGLU_LIMIT)
        up = jnp.clip(u, -SWIGLU_LIMIT, SWIGLU_LIMIT)
        act = (up + 1.0) * gate * _sigmoid(SWIGLU_ALPHA * gate)
        o_ref[...] = act.astype(BF16)

    @pl.when(i >= nu_ref[0])
    def _():
        o_ref[...] = jnp.zeros(o_ref.shape, BF16)


def _moe_gate_up(tile_e, n_used, xs, w_gu, b_gu):
    r, d = xs.shape
    nf = D_FF // MOE_TF
    grid_spec = pltpu.PrefetchScalarGridSpec(
        num_scalar_prefetch=2,
        grid=(nf, r // MOE_TM),
        in_specs=[
            pl.BlockSpec((MOE_TM, d), lambda f, i, te, nu: (i, 0)),
            pl.BlockSpec((1, d, MOE_TF), lambda f, i, te, nu: (te[i], 0, f)),
            pl.BlockSpec((1, d, MOE_TF), lambda f, i, te, nu: (te[i], 0, nf + f)),
            pl.BlockSpec((1, 1, MOE_TF), lambda f, i, te, nu: (te[i], 0, f)),
            pl.BlockSpec((1, 1, MOE_TF), lambda f, i, te, nu: (te[i], 0, nf + f)),
        ],
        out_specs=pl.BlockSpec((MOE_TM, MOE_TF), lambda f, i, te, nu: (i, f)),
    )
    return pl.pallas_call(
        _moe_gu_kernel,
        out_shape=jax.ShapeDtypeStruct((r, D_FF), BF16),
        grid_spec=grid_spec,
        compiler_params=_cparams(("arbitrary", "arbitrary")),
        name="moe_gate_up",
    )(tile_e, n_used, xs, w_gu, w_gu, b_gu, b_gu)


def _moe_down_kernel(te_ref, nu_ref, a_ref, wd_ref, bd_ref, o_ref):
    i = pl.program_id(1)

    @pl.when(i < nu_ref[0])
    def _():
        o_ref[...] = (jnp.dot(a_ref[...], wd_ref[0].astype(BF16), preferred_element_type=F32)
                      + bd_ref[0])

    @pl.when(i >= nu_ref[0])
    def _():
        o_ref[...] = jnp.zeros(o_ref.shape, F32)


def _moe_down(tile_e, n_used, act, w_d, b_d):
    r = act.shape[0]
    nn = D_MODEL // MOE_TN
    grid_spec = pltpu.PrefetchScalarGridSpec(
        num_scalar_prefetch=2,
        grid=(nn, r // MOE_TM),
        in_specs=[
            pl.BlockSpec((MOE_TM, D_FF), lambda n, i, te, nu: (i, 0)),
            pl.BlockSpec((1, D_FF, MOE_TN), lambda n, i, te, nu: (te[i], 0, n)),
            pl.BlockSpec((1, 1, MOE_TN), lambda n, i, te, nu: (te[i], 0, n)),
        ],
        out_specs=pl.BlockSpec((MOE_TM, MOE_TN), lambda n, i, te, nu: (i, n)),
    )
    return pl.pallas_call(
        _moe_down_kernel,
        out_shape=jax.ShapeDtypeStruct((r, D_MODEL), F32),
        grid_spec=grid_spec,
        compiler_params=_cparams(("arbitrary", "arbitrary")),
        name="moe_down",
    )(tile_e, n_used, act, w_d, b_d)


def _final_kernel(pos_ref, x1_ref, w_ref, g2_ref, gf_ref, y_hbm, o_ref, buf, sem):
    m = pl.program_id(0)
    nm = pl.num_programs(0)
    tm = x1_ref.shape[0]

    def issue(step, slot):
        base = step * (tm * TOP_K)

        def body(t, carry):
            for k in range(TOP_K):
                r = pos_ref[base + t * TOP_K + k]
                pltpu.make_async_copy(y_hbm.at[pl.ds(r, 1), :],
                                      buf.at[slot, k, pl.ds(t, 1), :], sem.at[slot]).start()
            return carry

        lax.fori_loop(0, tm, body, 0)

    @pl.when(m == 0)
    def _():
        issue(0, 0)

    @pl.when(m + 1 < nm)
    def _():
        issue(m + 1, (m + 1) % 2)

    slot = m % 2
    for k in range(TOP_K):
        pltpu.make_async_copy(y_hbm.at[pl.ds(0, tm), :], buf.at[slot, k], sem.at[slot]).wait()
    w = w_ref[...]
    moe = w[:, 0:1] * buf[slot, 0]
    for k in range(1, TOP_K):
        moe = moe + w[:, k:k + 1] * buf[slot, k]
    x2 = x1_ref[...] + g2_ref[...] * moe
    o_ref[...] = x2 * lax.rsqrt(jnp.mean(x2 * x2, axis=-1, keepdims=True) + EPS) * gf_ref[...]


def _final(pos_flat, x1, w_top, g2, gf, ys):
    s, d = x1.shape
    tm = FIN_TM
    const = lambda m, pos: (0, 0)
    grid_spec = pltpu.PrefetchScalarGridSpec(
        num_scalar_prefetch=1,
        grid=(s // tm,),
        in_specs=[
            pl.BlockSpec((tm, d), lambda m, pos: (m, 0)),
            pl.BlockSpec((tm, LANES), lambda m, pos: (m, 0)),
            pl.BlockSpec((1, d), const),
            pl.BlockSpec((1, d), const),
            pl.BlockSpec(memory_space=pl.ANY),
        ],
        out_specs=pl.BlockSpec((tm, d), lambda m, pos: (m, 0)),
        scratch_shapes=[pltpu.VMEM((2, TOP_K, tm, d), F32), pltpu.SemaphoreType.DMA((2,))],
    )
    return pl.pallas_call(
        _final_kernel,
        out_shape=jax.ShapeDtypeStruct((s, d), F32),
        grid_spec=grid_spec,
        compiler_params=_cparams(("arbitrary",)),
        name="final_combine",
    )(pos_flat, x1, w_top, g2, gf, ys)


ROUTE_TM = 512


def _route_kernel(lg_ref, pos_ref, w_ref, te_ref, nu_ref, cnt_s, pstart_s, carry_s):
    p = pl.program_id(0)
    m = pl.program_id(1)
    tm = lg_ref.shape[0]
    lane = lax.broadcasted_iota(jnp.int32, (tm, LANES), 1)
    lanef = lane.astype(F32)
    l = jnp.where(lane < N_EXPERTS, lg_ref[...], -jnp.inf)
    sel = jnp.zeros((tm, LANES), F32)
    vals, hots = [], []
    for _ in range(TOP_K):
        mx = jnp.max(l, axis=1, keepdims=True)
        idx = jnp.min(jnp.where(l == mx, lanef, float(LANES)), axis=1, keepdims=True)
        hot = lanef == idx
        vals.append(mx)
        hots.append(hot)
        l = jnp.where(hot, -jnp.inf, l)
        sel = sel + jnp.where(hot, 1.0, 0.0)
    colsum = jnp.sum(sel, axis=0, keepdims=True)

    @pl.when(p == 0)
    def _():
        @pl.when(m == 0)
        def _():
            cnt_s[...] = jnp.zeros(cnt_s.shape, F32)

        cnt_s[...] = cnt_s[...] + colsum

    @pl.when(jnp.logical_and(p == 1, m == 0))
    def _():
        cnt = cnt_s[...]
        padded = jnp.floor((cnt + (MOE_TM - 1.0)) * (1.0 / MOE_TM)) * MOE_TM
        lane8 = lax.broadcasted_iota(jnp.int32, cnt.shape, 1)
        pend = padded
        d = 1
        while d < LANES:
            pend = pend + jnp.where(lane8 >= d, pltpu.roll(pend, d, axis=1), 0.0)
            d *= 2
        pstart_s[...] = pend - padded
        carry_s[...] = jnp.zeros(carry_s.shape, F32)
        total = pend[:, N_EXPERTS - 1:N_EXPERTS]
        nt = te_ref.shape[0]
        tstart = lax.broadcasted_iota(jnp.int32, (nt, LANES), 0).astype(F32) * MOE_TM
        lane_t = lax.broadcasted_iota(jnp.int32, (nt, LANES), 1)
        done = jnp.where(lane_t < N_EXPERTS, jnp.where(tstart >= pend[0:1, :], 1.0, 0.0), 0.0)
        te = jnp.sum(done, axis=1, keepdims=True)
        last = jnp.sum(jnp.where(lane8 < N_EXPERTS,
                                 jnp.where(pend <= total - 1.0, 1.0, 0.0), 0.0),
                       axis=1, keepdims=True)
        te_ref[...] = jnp.minimum(te, last[0:1, :]).astype(jnp.int32)
        nu_ref[...] = jnp.broadcast_to(total * (1.0 / MOE_TM), nu_ref.shape).astype(jnp.int32)

    @pl.when(p == 1)
    def _():
        r = lax.broadcasted_iota(jnp.int32, (tm, tm), 0)
        c = lax.broadcasted_iota(jnp.int32, (tm, tm), 1)
        tri = jnp.where(r > c, 1.0, 0.0).astype(BF16)
        rowbase = (jnp.dot(tri, sel.astype(BF16), preferred_element_type=F32)
                   + carry_s[0:1, :] + pstart_s[0:1, :])
        carry_s[...] = carry_s[...] + colsum
        den = jnp.exp(vals[0] - vals[0])
        for k in range(1, TOP_K):
            den = den + jnp.exp(vals[k] - vals[0])
        posf = jnp.zeros((tm, LANES), F32)
        wf = jnp.zeros((tm, LANES), F32)
        for k in range(TOP_K):
            pk = jnp.sum(jnp.where(hots[k], rowbase, 0.0), axis=1, keepdims=True)
            wk = jnp.exp(vals[k] - vals[0]) / den
            posf = jnp.where(lane == k, pk, posf)
            wf = jnp.where(lane == k, wk, wf)
        pos_ref[...] = posf.astype(jnp.int32)
        w_ref[...] = wf


def _route(lg, n_tiles):
    s = lg.shape[0]
    tm = ROUTE_TM
    nt_pad = -(-n_tiles // 8) * 8
    const = lambda p, m: (0, 0)
    return pl.pallas_call(
        _route_kernel,
        out_shape=(jax.ShapeDtypeStruct((s, LANES), jnp.int32),
                   jax.ShapeDtypeStruct((s, LANES), F32),
                   jax.ShapeDtypeStruct((nt_pad, 1), jnp.int32),
                   jax.ShapeDtypeStruct((8, LANES), jnp.int32)),
        grid=(2, s // tm),
        in_specs=[pl.BlockSpec((tm, LANES), lambda p, m: (m, 0))],
        out_specs=(pl.BlockSpec((tm, LANES), lambda p, m: (m * p, 0)),
                   pl.BlockSpec((tm, LANES), lambda p, m: (m * p, 0)),
                   pl.BlockSpec((nt_pad, 1), const),
                   pl.BlockSpec((8, LANES), const)),
        scratch_shapes=[pltpu.VMEM((8, LANES), F32), pltpu.VMEM((8, LANES), F32),
                        pltpu.VMEM((8, LANES), F32)],
        compiler_params=_cparams(("arbitrary", "arbitrary")),
        name="route",
    )(lg)


DISP_TM = 256


def _dispatch_kernel(pos_ref, h_ref, xs_in, xs_hbm, sem):
    del xs_in
    m = pl.program_id(0)
    tm = h_ref.shape[0]
    base = m * (tm * TOP_K)

    def body(t, carry):
        for k in range(TOP_K):
            r = pos_ref[base + t * TOP_K + k]
            pltpu.make_async_copy(h_ref.at[pl.ds(t, 1), :], xs_hbm.at[pl.ds(r, 1), :], sem).start()
        return carry

    lax.fori_loop(0, tm, body, 0)
    for _ in range(TOP_K):
        pltpu.make_async_copy(h_ref, xs_hbm.at[pl.ds(0, tm), :], sem).wait()


def _dispatch(pos_flat, h2, xs_zero):
    s, d = h2.shape
    tm = DISP_TM
    grid_spec = pltpu.PrefetchScalarGridSpec(
        num_scalar_prefetch=1,
        grid=(s // tm,),
        in_specs=[pl.BlockSpec((tm, d), lambda m, pos: (m, 0)),
                  pl.BlockSpec(memory_space=pl.ANY)],
        out_specs=pl.BlockSpec(memory_space=pl.ANY),
        scratch_shapes=[pltpu.SemaphoreType.DMA(())],
    )
    return pl.pallas_call(
        _dispatch_kernel,
        out_shape=jax.ShapeDtypeStruct(xs_zero.shape, F32),
        grid_spec=grid_spec,
        input_output_aliases={2: 0},
        compiler_params=_cparams(("arbitrary",)),
        name="dispatch",
    )(pos_flat, h2, xs_zero)


def _pack_w_in(w_in):
    sizes = (M_QK_W, M_QK_W, M_V_W, M_HEADS, M_HEADS, M_V_W, F_W, F_W, F_W, F_HEADS, F_W,
             D_MODEL, D_MODEL)
    offs = [0]
    for sz in sizes:
        offs.append(offs[-1] + sz)
    seg = lambda i: w_in[:, offs[i]:offs[i + 1]]
    packed = jnp.concatenate([seg(11), seg(12), seg(0), seg(1), seg(2), seg(5), seg(6), seg(7),
                              seg(8), seg(10)], axis=1).astype(BF16)
    gates = jnp.concatenate([seg(3), seg(4), seg(9),
                             jnp.zeros((w_in.shape[0], LANES - 3 * M_HEADS), F32)], axis=1)
    return packed, gates


def kernel(x, c, ada_w, ada_b, norm1_g, w_in, m_conv_w, m_conv_b, m_i_bias, m_f_bias,
           m_out_norm_g, f_f_bias, f_q_norm_g, f_k_norm_g, w_branch_m, w_branch_f, w_out,
           norm2_g, w_router, b_router, w_gate_up, b_gate_up, w_down, b_down, final_norm_g):
    b, s, d = x.shape
    assert b == 1 and ada_w.shape[0] == 1
    x2d = x.reshape(s, d)

    mod = _ada_mod(c.reshape(d, 1), ada_w.reshape(d, 6 * d), ada_b.reshape(1, 6 * d))
    sh1, sc1, g1, sh2, sc2, g2 = [mod[:, i * d:(i + 1) * d] for i in range(6)]

    w_pk, w_gates = _pack_w_in(w_in.reshape(d, -1))
    z, zg = _inproj(x2d, norm1_g.reshape(1, d), sc1, sh1, w_pk, w_gates,
                    f_q_norm_g.reshape(1, F_W), f_k_norm_g.reshape(1, F_W))

    i_pre = zg[:, 0:M_HEADS] + m_i_bias.reshape(1, M_HEADS)
    logf_m = jax.nn.log_sigmoid(zg[:, M_HEADS:2 * M_HEADS] + m_f_bias.reshape(1, M_HEADS))
    logf_f = jax.nn.log_sigmoid(zg[:, 2 * M_HEADS:3 * M_HEADS] + f_f_bias.reshape(1, F_HEADS))
    bm = jnp.cumsum(logf_m, axis=0)
    ff = jnp.cumsum(logf_f, axis=0)

    hm = _mlstm(z, bm, i_pre, bm.T, i_pre.T, m_conv_w.reshape(M_CONV, 2 * M_QK_W),
                m_conv_b.reshape(1, 2 * M_QK_W), m_out_norm_g.reshape(1, M_V_W))
    hf = _fox(z, (-ff.T).reshape(F_HEADS, 1, s))

    wr = jnp.concatenate([w_router.reshape(d, N_EXPERTS),
                          jnp.zeros((d, LANES - N_EXPERTS), F32)], axis=1)
    br = jnp.concatenate([b_router.reshape(1, N_EXPERTS),
                          jnp.zeros((1, LANES - N_EXPERTS), F32)], axis=1)
    x1, h2, lg = _outproj(hm, hf, z, x2d,
                          w_branch_m.reshape(M_V_W, d).astype(BF16),
                          w_branch_f.reshape(F_W, d).astype(BF16),
                          w_out.reshape(d, d).astype(BF16),
                          g1, norm2_g.reshape(1, d), sc2, sh2, wr, br)

    n_rows = s * TOP_K + N_EXPERTS * MOE_TM
    n_tiles = n_rows // MOE_TM
    pos, w_top, te, nu = _route(lg, n_tiles)
    pos_flat = pos[:, :TOP_K].reshape(s * TOP_K)
    tile_e = te[:n_tiles, 0]
    n_used = nu[0, :1]
    xs = _dispatch(pos_flat, h2, jnp.zeros((n_rows, d), F32))
    act = _moe_gate_up(tile_e, n_used, xs, w_gate_up.reshape(N_EXPERTS, d, 2 * D_FF),
                       b_gate_up.reshape(N_EXPERTS, 1, 2 * D_FF))
    ys = _moe_down(tile_e, n_used, act, w_down.reshape(N_EXPERTS, D_FF, d),
                   b_down.reshape(N_EXPERTS, 1, d))
    out = _final(pos_flat, x1, w_top, g2, final_norm_g.reshape(1, d), ys)
    return out.reshape(b, s, d)
```

```python
import functools

import jax
import jax.numpy as jnp
from jax import lax
from jax.experimental import pallas as pl
from jax.experimental.pallas import tpu as pltpu

F32 = jnp.float32
BF16 = jnp.bfloat16

D_MODEL = 2048
M_HEADS = 8
M_DQK = 64
M_DV = 128
M_CONV = 4
F_HEADS = 8
F_DH = 128
N_EXPERTS = 32
TOP_K = 4
D_FF = 2048
SWIGLU_LIMIT = 7.0
SWIGLU_ALPHA = 1.702
EPS = 1e-6
LOG2E = 1.4426950408889634

M_QK_W = M_HEADS * M_DQK
M_V_W = M_HEADS * M_DV
F_W = F_HEADS * F_DH

VMEM_LIMIT = 56 * 1024 * 1024
LANES = 128

PK_BLOCK = 1024
PK_NBLK = 11
BLK_GA, BLK_GB, BLK_MQK, BLK_MV, BLK_MO, BLK_FQ, BLK_FK, BLK_FV, BLK_FO = 0, 2, 4, 5, 6, 7, 8, 9, 10

IN_TM = 1024
MLSTM_L = 256
FOX_T = 512
OUT_TM = 256
MOE_TM = 256
MOE_TF = 1024
MOE_TN = 1024
FIN_TM = 256


def _cparams(sem):
    return pltpu.CompilerParams(dimension_semantics=sem, vmem_limit_bytes=VMEM_LIMIT)


def _sigmoid(x):
    return 1.0 / (1.0 + jnp.exp(-x))


ADA_TN = 1024


def _ada_kernel(c_ref, w_ref, b_ref, o_ref):
    nrow = w_ref.shape[0] // 8

    def body(r, acc):
        rows = pl.ds(pl.multiple_of(r * 8, 8), 8)
        c = c_ref[rows, :]
        sc = c * _sigmoid(c)
        return acc + sc * w_ref[rows, :]

    acc = lax.fori_loop(0, nrow, body, jnp.zeros((8, w_ref.shape[1]), F32), unroll=8)
    o_ref[...] = jnp.sum(acc, axis=0, keepdims=True) + b_ref[...]


def _ada_mod(c_col, ada_w, ada_b):
    d, n = ada_w.shape
    return pl.pallas_call(
        _ada_kernel,
        out_shape=jax.ShapeDtypeStruct((1, n), F32),
        grid=(n // ADA_TN,),
        in_specs=[
            pl.BlockSpec((d, 1), lambda j: (0, 0)),
            pl.BlockSpec((d, ADA_TN), lambda j: (0, j)),
            pl.BlockSpec((1, ADA_TN), lambda j: (0, j)),
        ],
        out_specs=pl.BlockSpec((1, ADA_TN), lambda j: (0, j)),
        compiler_params=_cparams(("arbitrary",)),
        name="ada_mod",
    )(c_col, ada_w, ada_b)


def _head_rmsnorm(a, g_row, scale):
    outs = []
    for h in range(F_HEADS):
        ah = a[:, h * F_DH:(h + 1) * F_DH]
        ms = jnp.mean(ah * ah, axis=-1, keepdims=True)
        outs.append(ah * lax.rsqrt(ms + EPS) * (g_row[:, h * F_DH:(h + 1) * F_DH] * scale))
    return jnp.concatenate(outs, axis=-1)


def _inproj_kernel(x_ref, n1_ref, sc1_ref, sh1_ref, w_ref, wg_ref, qg_ref, kg_ref,
                   z_ref, zg_ref, h_s):
    n = pl.program_id(1)

    @pl.when(n == 0)
    def _():
        x = x_ref[...]
        y = x * lax.rsqrt(jnp.mean(x * x, axis=-1, keepdims=True) + EPS) * n1_ref[...]
        h = y * (1.0 + sc1_ref[...]) + sh1_ref[...]
        h_s[...] = h.astype(BF16)
        zg_ref[...] = jnp.dot(h, wg_ref[...], preferred_element_type=F32,
                              precision=lax.Precision.HIGHEST)

    acc = jnp.dot(h_s[...], w_ref[...], preferred_element_type=F32)

    @pl.when(n == BLK_FQ)
    def _():
        z_ref[...] = _head_rmsnorm(acc, qg_ref[...], F_DH ** -0.5 * LOG2E).astype(BF16)

    @pl.when(n == BLK_FK)
    def _():
        z_ref[...] = _head_rmsnorm(acc, kg_ref[...], 1.0).astype(BF16)

    @pl.when(jnp.logical_and(n != BLK_FQ, n != BLK_FK))
    def _():
        z_ref[...] = acc.astype(BF16)


def _inproj(x, n1, sc1, sh1, w_pk, w_gates, qg, kg):
    s, d = x.shape
    row = lambda m, n: (0, 0)
    return pl.pallas_call(
        _inproj_kernel,
        out_shape=(jax.ShapeDtypeStruct((s, PK_NBLK * PK_BLOCK), BF16),
                   jax.ShapeDtypeStruct((s, LANES), F32)),
        grid=(s // IN_TM, PK_NBLK),
        in_specs=[
            pl.BlockSpec((IN_TM, d), lambda m, n: (m, 0)),
            pl.BlockSpec((1, d), row),
            pl.BlockSpec((1, d), row),
            pl.BlockSpec((1, d), row),
            pl.BlockSpec((d, PK_BLOCK), lambda m, n: (0, n)),
            pl.BlockSpec((d, LANES), row),
            pl.BlockSpec((1, PK_BLOCK), row),
            pl.BlockSpec((1, PK_BLOCK), row),
        ],
        out_specs=(pl.BlockSpec((IN_TM, PK_BLOCK), lambda m, n: (m, n)),
                   pl.BlockSpec((IN_TM, LANES), lambda m, n: (m, 0))),
        scratch_shapes=[pltpu.VMEM((IN_TM, d), BF16)],
        compiler_params=_cparams(("arbitrary", "arbitrary")),
        name="inproj",
    )(x, n1, sc1, sh1, w_pk, w_gates, qg, kg)


G_IPRE, G_BM, G_FF = 0, M_HEADS, 2 * M_HEADS
G_ROWS = 3 * M_HEADS
GATE_TM = 512


def _gates_kernel(zg_ref, b_ref, gcol_ref, grow_ref, carry_s):
    t = pl.program_id(0)
    tm = zg_ref.shape[0]

    @pl.when(t == 0)
    def _():
        carry_s[...] = jnp.zeros(carry_s.shape, F32)

    a = zg_ref[...] + b_ref[...]
    lane = lax.broadcasted_iota(jnp.int32, a.shape, 1)
    logf = jnp.minimum(a, 0.0) - jnp.log1p(jnp.exp(-jnp.abs(a)))
    is_f = jnp.logical_and(lane >= G_BM, lane < G_ROWS)
    r = lax.broadcasted_iota(jnp.int32, (tm, tm), 0)
    c = lax.broadcasted_iota(jnp.int32, (tm, tm), 1)
    tri = jnp.where(r >= c, 1.0, 0.0)
    cum = jnp.dot(tri, jnp.where(is_f, logf, 0.0), preferred_element_type=F32,
                  precision=lax.Precision.HIGHEST) + carry_s[0:1, :]
    carry_s[...] = jnp.broadcast_to(cum[tm - 1:tm, :], carry_s.shape)
    g = jnp.where(lane < G_BM, a, cum)
    gcol_ref[...] = g
    grow_ref[...] = g.T[0:G_ROWS, :]


def _gates(zg, bias_row):
    s = zg.shape[0]
    tm = GATE_TM
    return pl.pallas_call(
        _gates_kernel,
        out_shape=(jax.ShapeDtypeStruct((s, LANES), F32),
                   jax.ShapeDtypeStruct((G_ROWS, s), F32)),
        grid=(s // tm,),
        in_specs=[pl.BlockSpec((tm, LANES), lambda t: (t, 0)),
                  pl.BlockSpec((1, LANES), lambda t: (0, 0))],
        out_specs=(pl.BlockSpec((tm, LANES), lambda t: (t, 0)),
                   pl.BlockSpec((G_ROWS, tm), lambda t: (0, t))),
        scratch_shapes=[pltpu.VMEM((8, LANES), F32)],
        compiler_params=_cparams(("arbitrary",)),
        name="gates",
    )(zg, bias_row)


def _mlstm_kernel(qk_ref, v_ref, og_ref, gcol_ref, irow_ref, brow_ref,
                  cw_ref, cb_ref, gn_ref, out_ref, xbuf, c_s, m_s, bend_s):
    c = pl.program_id(0)
    L = qk_ref.shape[0]

    @pl.when(c == 0)
    def _():
        xbuf[0:8, :] = jnp.zeros((8, xbuf.shape[1]), F32)
        c_s[...] = jnp.zeros(c_s.shape, F32)
        m_s[...] = jnp.zeros(m_s.shape, F32)
        bend_s[...] = jnp.zeros(bend_s.shape, F32)

    xbuf[8:8 + L, :] = qk_ref[...].astype(F32)
    y = cb_ref[...] + cw_ref[0:1, :] * xbuf[5:5 + L, :]
    for j in range(1, M_CONV):
        y = y + cw_ref[j:j + 1, :] * xbuf[5 + j:5 + j + L, :]
    xbuf[0:8, :] = xbuf[L:L + 8, :]
    y = y * _sigmoid(y)

    rows = lax.broadcasted_iota(jnp.int32, (L, L), 0)
    cols = lax.broadcasted_iota(jnp.int32, (L, L), 1)
    causal = rows >= cols
    lane = lax.broadcasted_iota(jnp.int32, (L, M_DV), 1)
    ones_blk = jnp.where(lane == 0, 1.0, 0.0).astype(BF16)

    for h in range(M_HEADS):
        q = y[:, h * M_DQK:(h + 1) * M_DQK].astype(BF16)
        kf = y[:, M_QK_W + h * M_DQK:M_QK_W + (h + 1) * M_DQK] * (M_DQK ** -0.5)
        k = kf.astype(BF16)
        vext = jnp.concatenate([v_ref[:, h * M_DV:(h + 1) * M_DV], ones_blk], axis=-1)
        icol = gcol_ref[:, G_IPRE + h:G_IPRE + h + 1]
        bcol = gcol_ref[:, G_BM + h:G_BM + h + 1]
        rowterm = irow_ref[h:h + 1, :] - brow_ref[h:h + 1, :]
        bprev = bend_s[h:h + 1, 0:1]
        mprev = m_s[h:h + 1, 0:1]

        dmat = jnp.where(causal, bcol + rowterm, -jnp.inf)
        inter = (bcol - bprev) + mprev
        m_t = jnp.maximum(inter, jnp.max(dmat, axis=1, keepdims=True))
        w_intra = jnp.exp(dmat - m_t)
        w_inter = jnp.exp(inter - m_t)
        s = lax.dot_general(q, k, (((1,), (1,)), ((), ())), preferred_element_type=F32) * w_intra
        cx = c_s[h]
        nd = (jnp.dot(s.astype(BF16), vext, preferred_element_type=F32)
              + w_inter * jnp.dot(q, cx.astype(BF16), preferred_element_type=F32))
        num = nd[:, :M_DV]
        den = nd[:, M_DV:M_DV + 1]
        hv = num / jnp.maximum(jnp.abs(den), jnp.exp(-m_t))
        ms = jnp.mean(hv * hv, axis=-1, keepdims=True)
        hn = hv * lax.rsqrt(ms + EPS) * gn_ref[:, h * M_DV:(h + 1) * M_DV]
        og = og_ref[:, h * M_DV:(h + 1) * M_DV].astype(F32)
        out_ref[:, h * M_DV:(h + 1) * M_DV] = (hn * _sigmoid(og)).astype(BF16)

        bend = bcol[L - 1:L, :]
        gtot = bend - bprev
        acol = (bend - bcol) + icol
        m_new = jnp.maximum(gtot + mprev, jnp.max(acol, axis=0, keepdims=True))
        wk = jnp.exp(acol - m_new)
        decay = jnp.exp(gtot + mprev - m_new)
        kw = (kf * wk).astype(BF16)
        upd = lax.dot_general(kw, vext, (((0,), (0,)), ((), ())), preferred_element_type=F32)
        c_s[h] = decay * cx + upd
        m_s[h:h + 1, :] = jnp.broadcast_to(m_new, (1, LANES))
        bend_s[h:h + 1, :] = jnp.broadcast_to(bend, (1, LANES))


def _mlstm(z, gcol, grow, conv_w, conv_b, gnorm):
    s = z.shape[0]
    L = MLSTM_L
    const = lambda c: (0, 0)
    return pl.pallas_call(
        _mlstm_kernel,
        out_shape=jax.ShapeDtypeStruct((s, M_V_W), BF16),
        grid=(s // L,),
        in_specs=[
            pl.BlockSpec((L, PK_BLOCK), lambda c: (c, BLK_MQK)),
            pl.BlockSpec((L, PK_BLOCK), lambda c: (c, BLK_MV)),
            pl.BlockSpec((L, PK_BLOCK), lambda c: (c, BLK_MO)),
            pl.BlockSpec((L, LANES), lambda c: (c, 0)),
            pl.BlockSpec((M_HEADS, L), lambda c: (G_IPRE // M_HEADS, c)),
            pl.BlockSpec((M_HEADS, L), lambda c: (G_BM // M_HEADS, c)),
            pl.BlockSpec((M_CONV, 2 * M_QK_W), const),
            pl.BlockSpec((1, 2 * M_QK_W), const),
            pl.BlockSpec((1, M_V_W), const),
        ],
        out_specs=pl.BlockSpec((L, M_V_W), lambda c: (c, 0)),
        scratch_shapes=[
            pltpu.VMEM((L + 8, 2 * M_QK_W), F32),
            pltpu.VMEM((M_HEADS, M_DQK, 2 * M_DV), F32),
            pltpu.VMEM((M_HEADS, LANES), F32),
            pltpu.VMEM((M_HEADS, LANES), F32),
        ],
        compiler_params=_cparams(("arbitrary",)),
        name="mlstm",
    )(z, z, z, gcol, grow, grow, conv_w, conv_b, gnorm)


FOX_KX = 2 * F_DH
FOX_NPIECE = 3
FOX_VT = F_DH + 16
FOX_VT_ONE = F_DH


FOX_MASKED = -1e30


def _fox_kernel(q_ref, k_ref, v_ref, f_ref, og_ref, o_ref, kx_s, vt_s, s0, s1, p0, p1,
                acc_s, m_s):
    i = pl.program_id(1)
    T = q_ref.shape[0]
    S = k_ref.shape[0]

    @pl.when(i == 0)
    def _():
        lane = lax.broadcasted_iota(jnp.int32, (T, F_DH), 1)
        row = lax.broadcasted_iota(jnp.int32, (FOX_VT - F_DH, T), 0)
        ones_rows = jnp.where(row == 0, 1.0, 0.0).astype(BF16)

        def prep(c, carry):
            off = pl.multiple_of(c * T, T)
            fsel = jnp.where(lane == G_FF + pl.program_id(0), f_ref[pl.ds(off, T), :], 0.0)
            rem = -LOG2E * jnp.sum(fsel, axis=1, keepdims=True)
            ext = jnp.zeros((T, F_DH), F32)
            for piece in range(FOX_NPIECE):
                part = rem.astype(BF16).astype(F32)
                ext = jnp.where(lane == piece, part, ext)
                rem = rem - part
            kx_s[pl.ds(off, T), 0:F_DH] = k_ref[pl.ds(off, T), :]
            kx_s[pl.ds(off, T), F_DH:FOX_KX] = ext.astype(BF16)
            vt_s[0:F_DH, pl.ds(off, T)] = v_ref[pl.ds(off, T), :].astype(F32).T.astype(BF16)
            vt_s[F_DH:FOX_VT, pl.ds(off, T)] = ones_rows
            return carry

        lax.fori_loop(0, S // T, prep, 0)

    lane_q = lax.broadcasted_iota(jnp.int32, (T, F_DH), 1)
    qx = jnp.concatenate(
        [q_ref[...], jnp.where(lane_q < FOX_NPIECE, 1.0, 0.0).astype(BF16)], axis=1)

    def score(blk, s_buf, mask):
        off = pl.multiple_of(blk * T, T)
        st = lax.dot_general(kx_s[pl.ds(off, T), :], qx, (((1,), (1,)), ((), ())),
                             preferred_element_type=F32)
        if mask is not None:
            st = jnp.where(mask, st, FOX_MASKED)
        s_buf[...] = st
        return jnp.max(st, axis=0, keepdims=True)

    def soft(s_buf, p_buf, m, bmax):
        m_new = jnp.maximum(m, bmax)
        p_buf[...] = jnp.exp2(s_buf[...] - m_new).astype(BF16)
        return m_new, jnp.exp2(m - m_new)

    def accum(blk, p_buf, alpha):
        vt = vt_s[:, pl.ds(pl.multiple_of(blk * T, T), T)]
        acc_s[...] = alpha * acc_s[...] + jnp.dot(vt, p_buf[...], preferred_element_type=F32)

    keys = lax.broadcasted_iota(jnp.int32, (T, T), 0)
    queries = lax.broadcasted_iota(jnp.int32, (T, T), 1)
    causal = queries >= keys
    odd = i % 2
    acc_s[...] = jnp.zeros(acc_s.shape, F32)
    m_s[...] = jnp.full(m_s.shape, FOX_MASKED, F32)

    @pl.when(odd == 1)
    def _():
        bmax = score(i, s0, causal)
        m_new, alpha = soft(s0, p0, m_s[...], bmax)
        accum(i, p0, alpha)
        m_s[...] = m_new

    def block_of(u):
        return jnp.where(jnp.logical_and(odd == 0, u == 0), i, u - 1 + odd)

    last = i - odd
    first_mask = jnp.logical_or(causal, odd == 1)
    bmax = score(block_of(0), s0, first_mask)
    m, alpha = soft(s0, p0, m_s[...], bmax)
    bmax = score(block_of(jnp.minimum(1, last)), s1, None)

    def body(d, carry):
        m, alpha, bmax = carry
        u = 2 * d
        accum(block_of(u), p0, alpha)
        m, alpha = soft(s1, p1, m, bmax)
        bmax = score(block_of(u + 2), s0, None)
        accum(block_of(u + 1), p1, alpha)
        m, alpha = soft(s0, p0, m, bmax)
        bmax = score(block_of(jnp.minimum(u + 3, last)), s1, None)
        return m, alpha, bmax

    m, alpha, _ = lax.fori_loop(0, last // 2, body, (m, alpha, bmax))
    accum(block_of(last), p0, alpha)
    acc = acc_s[...]
    out_t = acc[0:F_DH, :] / acc[FOX_VT_ONE:FOX_VT_ONE + 1, :]
    o_ref[...] = (out_t.T * _sigmoid(og_ref[...].astype(F32))).astype(BF16)


def _fox(z, fcum):
    s = z.shape[0]
    T = FOX_T
    cpb = PK_BLOCK // F_DH
    return pl.pallas_call(
        _fox_kernel,
        out_shape=jax.ShapeDtypeStruct((s, F_W), BF16),
        grid=(F_HEADS, s // T),
        in_specs=[
            pl.BlockSpec((T, F_DH), lambda h, i: (i, BLK_FQ * cpb + h)),
            pl.BlockSpec((s, F_DH), lambda h, i: (0, BLK_FK * cpb + h)),
            pl.BlockSpec((s, F_DH), lambda h, i: (0, BLK_FV * cpb + h)),
            pl.BlockSpec((s, LANES), lambda h, i: (0, 0)),
            pl.BlockSpec((T, F_DH), lambda h, i: (i, BLK_FO * cpb + h)),
        ],
        out_specs=pl.BlockSpec((T, F_DH), lambda h, i: (i, h)),
        scratch_shapes=[pltpu.VMEM((s, FOX_KX), BF16), pltpu.VMEM((FOX_VT, s), BF16),
                        pltpu.VMEM((T, T), F32), pltpu.VMEM((T, T), F32),
                        pltpu.VMEM((T, T), BF16), pltpu.VMEM((T, T), BF16),
                        pltpu.VMEM((FOX_VT, T), F32), pltpu.VMEM((1, T), F32)],
        compiler_params=_cparams(("arbitrary", "arbitrary")),
        name="fox_attention",
    )(z, z, z, fcum, z)


def _outproj_kernel(hm_ref, hf_ref, ga_ref, gb_ref, x_ref, wm_ref, wf_ref, wo_ref,
                    g1_ref, n2_ref, sc2_ref, sh2_ref, wr_ref, br_ref,
                    x1_ref, h2_ref, lg_ref):
    ym = jnp.dot(hm_ref[...], wm_ref[...], preferred_element_type=F32)
    yf = jnp.dot(hf_ref[...], wf_ref[...], preferred_element_type=F32)
    y = (_sigmoid(ga_ref[...].astype(F32)) * ym + _sigmoid(gb_ref[...].astype(F32)) * yf)
    mix = jnp.dot(y.astype(BF16), wo_ref[...], preferred_element_type=F32)
    x1 = x_ref[...] + g1_ref[...] * mix
    x1_ref[...] = x1
    hn = x1 * lax.rsqrt(jnp.mean(x1 * x1, axis=-1, keepdims=True) + EPS) * n2_ref[...]
    h2 = hn * (1.0 + sc2_ref[...]) + sh2_ref[...]
    h2_ref[...] = h2
    lg_ref[...] = jnp.dot(h2, wr_ref[...], preferred_element_type=F32,
                          precision=lax.Precision.HIGHEST) + br_ref[...]


def _outproj(hm, hf, z, x, wm, wf, wo, g1, n2, sc2, sh2, wr, br):
    s, d = x.shape
    tm = OUT_TM
    const = lambda m: (0, 0)
    gw = 2 * PK_BLOCK
    return pl.pallas_call(
        _outproj_kernel,
        out_shape=(jax.ShapeDtypeStruct((s, d), F32),
                   jax.ShapeDtypeStruct((s, d), F32),
                   jax.ShapeDtypeStruct((s, LANES), F32)),
        grid=(s // tm,),
        in_specs=[
            pl.BlockSpec((tm, M_V_W), lambda m: (m, 0)),
            pl.BlockSpec((tm, F_W), lambda m: (m, 0)),
            pl.BlockSpec((tm, gw), lambda m: (m, BLK_GA // 2)),
            pl.BlockSpec((tm, gw), lambda m: (m, BLK_GB // 2)),
            pl.BlockSpec((tm, d), lambda m: (m, 0)),
            pl.BlockSpec((M_V_W, d), const, pipeline_mode=pl.Buffered(1)),
            pl.BlockSpec((F_W, d), const, pipeline_mode=pl.Buffered(1)),
            pl.BlockSpec((d, d), const, pipeline_mode=pl.Buffered(1)),
            pl.BlockSpec((1, d), const),
            pl.BlockSpec((1, d), const),
            pl.BlockSpec((1, d), const),
            pl.BlockSpec((1, d), const),
            pl.BlockSpec((d, LANES), const),
            pl.BlockSpec((1, LANES), const),
        ],
        out_specs=(pl.BlockSpec((tm, d), lambda m: (m, 0)),
                   pl.BlockSpec((tm, d), lambda m: (m, 0)),
                   pl.BlockSpec((tm, LANES), lambda m: (m, 0))),
        compiler_params=_cparams(("arbitrary",)),
        name="outproj",
    )(hm, hf, z, z, x, wm, wf, wo, g1, n2, sc2, sh2, wr, br)


def _moe_gu_kernel(te_ref, nu_ref, x_ref, wg_ref, wu_ref, bg_ref, bu_ref, o_ref):
    i = pl.program_id(1)

    @pl.when(i < nu_ref[0])
    def _():
        x = x_ref[...].astype(BF16)
        g = jnp.dot(x, wg_ref[0].astype(BF16), preferred_element_type=F32) + bg_ref[0]
        u = jnp.dot(x, wu_ref[0].astype(BF16), preferred_element_type=F32) + bu_ref[0]
        gate = jnp.minimum(g, SWIGLU_LIMIT)
        up = jnp.clip(u, -SWIGLU_LIMIT, SWIGLU_LIMIT)
        act = (up + 1.0) * gate * _sigmoid(SWIGLU_ALPHA * gate)
        o_ref[...] = act.astype(BF16)

    @pl.when(i >= nu_ref[0])
    def _():
        o_ref[...] = jnp.zeros(o_ref.shape, BF16)


def _moe_gate_up(tile_e, n_used, xs, w_gu, b_gu):
    r, d = xs.shape
    nf = D_FF // MOE_TF
    grid_spec = pltpu.PrefetchScalarGridSpec(
        num_scalar_prefetch=2,
        grid=(nf, r // MOE_TM),
        in_specs=[
            pl.BlockSpec((MOE_TM, d), lambda f, i, te, nu: (i, 0)),
            pl.BlockSpec((1, d, MOE_TF), lambda f, i, te, nu: (te[i], 0, f)),
            pl.BlockSpec((1, d, MOE_TF), lambda f, i, te, nu: (te[i], 0, nf + f)),
            pl.BlockSpec((1, 1, MOE_TF), lambda f, i, te, nu: (te[i], 0, f)),
            pl.BlockSpec((1, 1, MOE_TF), lambda f, i, te, nu: (te[i], 0, nf + f)),
        ],
        out_specs=pl.BlockSpec((MOE_TM, MOE_TF), lambda f, i, te, nu: (i, f)),
    )
    return pl.pallas_call(
        _moe_gu_kernel,
        out_shape=jax.ShapeDtypeStruct((r, D_FF), BF16),
        grid_spec=grid_spec,
        compiler_params=_cparams(("arbitrary", "arbitrary")),
        name="moe_gate_up",
    )(tile_e, n_used, xs, w_gu, w_gu, b_gu, b_gu)


def _moe_down_kernel(te_ref, nu_ref, a_ref, wd_ref, bd_ref, o_ref):
    i = pl.program_id(1)

    @pl.when(i < nu_ref[0])
    def _():
        o_ref[...] = (jnp.dot(a_ref[...], wd_ref[0].astype(BF16), preferred_element_type=F32)
                      + bd_ref[0])

    @pl.when(i >= nu_ref[0])
    def _():
        o_ref[...] = jnp.zeros(o_ref.shape, F32)


def _moe_down(tile_e, n_used, act, w_d, b_d):
    r = act.shape[0]
    nn = D_MODEL // MOE_TN
    grid_spec = pltpu.PrefetchScalarGridSpec(
        num_scalar_prefetch=2,
        grid=(nn, r // MOE_TM),
        in_specs=[
            pl.BlockSpec((MOE_TM, D_FF), lambda n, i, te, nu: (i, 0)),
            pl.BlockSpec((1, D_FF, MOE_TN), lambda n, i, te, nu: (te[i], 0, n)),
            pl.BlockSpec((1, 1, MOE_TN), lambda n, i, te, nu: (te[i], 0, n)),
        ],
        out_specs=pl.BlockSpec((MOE_TM, MOE_TN), lambda n, i, te, nu: (i, n)),
    )
    return pl.pallas_call(
        _moe_down_kernel,
        out_shape=jax.ShapeDtypeStruct((r, D_MODEL), F32),
        grid_spec=grid_spec,
        compiler_params=_cparams(("arbitrary", "arbitrary")),
        name="moe_down",
    )(tile_e, n_used, act, w_d, b_d)


def _final_kernel(pos_ref, x1_ref, w_ref, g2_ref, gf_ref, y_hbm, o_ref, buf, sem):
    m = pl.program_id(0)
    nm = pl.num_programs(0)
    tm = x1_ref.shape[0]

    def issue(step, slot):
        base = step * (tm * TOP_K)

        def body(t, carry):
            for k in range(TOP_K):
                r = pos_ref[base + t * TOP_K + k]
                pltpu.make_async_copy(y_hbm.at[pl.ds(r, 1), :],
                                      buf.at[slot, k, pl.ds(t, 1), :], sem.at[slot]).start()
            return carry

        lax.fori_loop(0, tm, body, 0)

    @pl.when(m == 0)
    def _():
        issue(0, 0)

    @pl.when(m + 1 < nm)
    def _():
        issue(m + 1, (m + 1) % 2)

    slot = m % 2
    for k in range(TOP_K):
        pltpu.make_async_copy(y_hbm.at[pl.ds(0, tm), :], buf.at[slot, k], sem.at[slot]).wait()
    w = w_ref[...]
    moe = w[:, 0:1] * buf[slot, 0]
    for k in range(1, TOP_K):
        moe = moe + w[:, k:k + 1] * buf[slot, k]
    x2 = x1_ref[...] + g2_ref[...] * moe
    o_ref[...] = x2 * lax.rsqrt(jnp.mean(x2 * x2, axis=-1, keepdims=True) + EPS) * gf_ref[...]


def _final(pos_flat, x1, w_top, g2, gf, ys):
    s, d = x1.shape
    tm = FIN_TM
    const = lambda m, pos: (0, 0)
    grid_spec = pltpu.PrefetchScalarGridSpec(
        num_scalar_prefetch=1,
        grid=(s // tm,),
        in_specs=[
            pl.BlockSpec((tm, d), lambda m, pos: (m, 0)),
            pl.BlockSpec((tm, LANES), lambda m, pos: (m, 0)),
            pl.BlockSpec((1, d), const),
            pl.BlockSpec((1, d), const),
            pl.BlockSpec(memory_space=pl.ANY),
        ],
        out_specs=pl.BlockSpec((tm, d), lambda m, pos: (m, 0)),
        scratch_shapes=[pltpu.VMEM((2, TOP_K, tm, d), F32), pltpu.SemaphoreType.DMA((2,))],
    )
    return pl.pallas_call(
        _final_kernel,
        out_shape=jax.ShapeDtypeStruct((s, d), F32),
        grid_spec=grid_spec,
        compiler_params=_cparams(("arbitrary",)),
        name="final_combine",
    )(pos_flat, x1, w_top, g2, gf, ys)


ROUTE_TM = 512


def _route_kernel(lg_ref, pos_ref, w_ref, te_ref, nu_ref, cnt_s, pstart_s, carry_s):
    p = pl.program_id(0)
    m = pl.program_id(1)
    tm = lg_ref.shape[0]
    lane = lax.broadcasted_iota(jnp.int32, (tm, LANES), 1)
    lanef = lane.astype(F32)
    l = jnp.where(lane < N_EXPERTS, lg_ref[...], -jnp.inf)
    sel = jnp.zeros((tm, LANES), F32)
    vals, hots = [], []
    for _ in range(TOP_K):
        mx = jnp.max(l, axis=1, keepdims=True)
        idx = jnp.min(jnp.where(l == mx, lanef, float(LANES)), axis=1, keepdims=True)
        hot = lanef == idx
        vals.append(mx)
        hots.append(hot)
        l = jnp.where(hot, -jnp.inf, l)
        sel = sel + jnp.where(hot, 1.0, 0.0)
    colsum = jnp.sum(sel, axis=0, keepdims=True)

    @pl.when(p == 0)
    def _():
        @pl.when(m == 0)
        def _():
            cnt_s[...] = jnp.zeros(cnt_s.shape, F32)

        cnt_s[...] = cnt_s[...] + colsum

    @pl.when(jnp.logical_and(p == 1, m == 0))
    def _():
        cnt = cnt_s[...]
        padded = jnp.floor((cnt + (MOE_TM - 1.0)) * (1.0 / MOE_TM)) * MOE_TM
        lane8 = lax.broadcasted_iota(jnp.int32, cnt.shape, 1)
        pend = padded
        d = 1
        while d < LANES:
            pend = pend + jnp.where(lane8 >= d, pltpu.roll(pend, d, axis=1), 0.0)
            d *= 2
        pstart_s[...] = pend - padded
        carry_s[...] = jnp.zeros(carry_s.shape, F32)
        total = pend[:, N_EXPERTS - 1:N_EXPERTS]
        nt = te_ref.shape[0]
        tstart = lax.broadcasted_iota(jnp.int32, (nt, LANES), 0).astype(F32) * MOE_TM
        lane_t = lax.broadcasted_iota(jnp.int32, (nt, LANES), 1)
        done = jnp.where(lane_t < N_EXPERTS, jnp.where(tstart >= pend[0:1, :], 1.0, 0.0), 0.0)
        te = jnp.sum(done, axis=1, keepdims=True)
        last = jnp.sum(jnp.where(lane8 < N_EXPERTS,
                                 jnp.where(pend <= total - 1.0, 1.0, 0.0), 0.0),
                       axis=1, keepdims=True)
        te_ref[...] = jnp.minimum(te, last[0:1, :]).astype(jnp.int32)
        nu_ref[...] = jnp.broadcast_to(total * (1.0 / MOE_TM), nu_ref.shape).astype(jnp.int32)

    @pl.when(p == 1)
    def _():
        r = lax.broadcasted_iota(jnp.int32, (tm, tm), 0)
        c = lax.broadcasted_iota(jnp.int32, (tm, tm), 1)
        tri = jnp.where(r > c, 1.0, 0.0).astype(BF16)
        rowbase = (jnp.dot(tri, sel.astype(BF16), preferred_element_type=F32)
                   + carry_s[0:1, :] + pstart_s[0:1, :])
        carry_s[...] = carry_s[...] + colsum
        den = jnp.exp(vals[0] - vals[0])
        for k in range(1, TOP_K):
            den = den + jnp.exp(vals[k] - vals[0])
        posf = jnp.zeros((tm, LANES), F32)
        wf = jnp.zeros((tm, LANES), F32)
        for k in range(TOP_K):
            pk = jnp.sum(jnp.where(hots[k], rowbase, 0.0), axis=1, keepdims=True)
            wk = jnp.exp(vals[k] - vals[0]) / den
            posf = jnp.where(lane == k, pk, posf)
            wf = jnp.where(lane == k, wk, wf)
        pos_ref[...] = posf.astype(jnp.int32)
        w_ref[...] = wf


def _route(lg, n_tiles):
    s = lg.shape[0]
    tm = ROUTE_TM
    nt_pad = -(-n_tiles // 8) * 8
    const = lambda p, m: (0, 0)
    return pl.pallas_call(
        _route_kernel,
        out_shape=(jax.ShapeDtypeStruct((s, LANES), jnp.int32),
                   jax.ShapeDtypeStruct((s, LANES), F32),
                   jax.ShapeDtypeStruct((nt_pad, 1), jnp.int32),
                   jax.ShapeDtypeStruct((8, LANES), jnp.int32)),
        grid=(2, s // tm),
        in_specs=[pl.BlockSpec((tm, LANES), lambda p, m: (m, 0))],
        out_specs=(pl.BlockSpec((tm, LANES), lambda p, m: (m * p, 0)),
                   pl.BlockSpec((tm, LANES), lambda p, m: (m * p, 0)),
                   pl.BlockSpec((nt_pad, 1), const),
                   pl.BlockSpec((8, LANES), const)),
        scratch_shapes=[pltpu.VMEM((8, LANES), F32), pltpu.VMEM((8, LANES), F32),
                        pltpu.VMEM((8, LANES), F32)],
        compiler_params=_cparams(("arbitrary", "arbitrary")),
        name="route",
    )(lg)


DISP_TM = 256


def _dispatch_kernel(pos_ref, h_ref, xs_in, xs_hbm, sem):
    del xs_in
    m = pl.program_id(0)
    tm = h_ref.shape[0]
    base = m * (tm * TOP_K)

    def body(t, carry):
        for k in range(TOP_K):
            r = pos_ref[base + t * TOP_K + k]
            pltpu.make_async_copy(h_ref.at[pl.ds(t, 1), :], xs_hbm.at[pl.ds(r, 1), :], sem).start()
        return carry

    lax.fori_loop(0, tm, body, 0)
    for _ in range(TOP_K):
        pltpu.make_async_copy(h_ref, xs_hbm.at[pl.ds(0, tm), :], sem).wait()


def _dispatch(pos_flat, h2, xs_zero):
    s, d = h2.shape
    tm = DISP_TM
    grid_spec = pltpu.PrefetchScalarGridSpec(
        num_scalar_prefetch=1,
        grid=(s // tm,),
        in_specs=[pl.BlockSpec((tm, d), lambda m, pos: (m, 0)),
                  pl.BlockSpec(memory_space=pl.ANY)],
        out_specs=pl.BlockSpec(memory_space=pl.ANY),
        scratch_shapes=[pltpu.SemaphoreType.DMA(())],
    )
    return pl.pallas_call(
        _dispatch_kernel,
        out_shape=jax.ShapeDtypeStruct(xs_zero.shape, F32),
        grid_spec=grid_spec,
        input_output_aliases={2: 0},
        compiler_params=_cparams(("arbitrary",)),
        name="dispatch",
    )(pos_flat, h2, xs_zero)


def _pack_w_in(w_in):
    sizes = (M_QK_W, M_QK_W, M_V_W, M_HEADS, M_HEADS, M_V_W, F_W, F_W, F_W, F_HEADS, F_W,
             D_MODEL, D_MODEL)
    offs = [0]
    for sz in sizes:
        offs.append(offs[-1] + sz)
    seg = lambda i: w_in[:, offs[i]:offs[i + 1]]
    packed = jnp.concatenate([seg(11), seg(12), seg(0), seg(1), seg(2), seg(5), seg(6), seg(7),
                              seg(8), seg(10)], axis=1).astype(BF16)
    gates = jnp.concatenate([seg(3), seg(4), seg(9),
                             jnp.zeros((w_in.shape[0], LANES - 3 * M_HEADS), F32)], axis=1)
    return packed, gates


def kernel(x, c, ada_w, ada_b, norm1_g, w_in, m_conv_w, m_conv_b, m_i_bias, m_f_bias,
           m_out_norm_g, f_f_bias, f_q_norm_g, f_k_norm_g, w_branch_m, w_branch_f, w_out,
           norm2_g, w_router, b_router, w_gate_up, b_gate_up, w_down, b_down, final_norm_g):
    b, s, d = x.shape
    assert b == 1 and ada_w.shape[0] == 1
    x2d = x.reshape(s, d)

    mod = _ada_mod(c.reshape(d, 1), ada_w.reshape(d, 6 * d), ada_b.reshape(1, 6 * d))
    sh1, sc1, g1, sh2, sc2, g2 = [mod[:, i * d:(i + 1) * d] for i in range(6)]

    w_pk, w_gates = _pack_w_in(w_in.reshape(d, -1))
    z, zg = _inproj(x2d, norm1_g.reshape(1, d), sc1, sh1, w_pk, w_gates,
                    f_q_norm_g.reshape(1, F_W), f_k_norm_g.reshape(1, F_W))

    gate_bias = jnp.concatenate([m_i_bias.reshape(1, M_HEADS), m_f_bias.reshape(1, M_HEADS),
                                 f_f_bias.reshape(1, F_HEADS),
                                 jnp.zeros((1, LANES - G_ROWS), F32)], axis=1)
    gcol, grow = _gates(zg, gate_bias)

    hm = _mlstm(z, gcol, grow, m_conv_w.reshape(M_CONV, 2 * M_QK_W),
                m_conv_b.reshape(1, 2 * M_QK_W), m_out_norm_g.reshape(1, M_V_W))
    hf = _fox(z, gcol)

    wr = jnp.concatenate([w_router.reshape(d, N_EXPERTS),
                          jnp.zeros((d, LANES - N_EXPERTS), F32)], axis=1)
    br = jnp.concatenate([b_router.reshape(1, N_EXPERTS),
                          jnp.zeros((1, LANES - N_EXPERTS), F32)], axis=1)
    x1, h2, lg = _outproj(hm, hf, z, x2d,
                          w_branch_m.reshape(M_V_W, d).astype(BF16),
                          w_branch_f.reshape(F_W, d).astype(BF16),
                          w_out.reshape(d, d).astype(BF16),
                          g1, norm2_g.reshape(1, d), sc2, sh2, wr, br)

    n_rows = s * TOP_K + N_EXPERTS * MOE_TM
    n_tiles = n_rows // MOE_TM
    pos, w_top, te, nu = _route(lg, n_tiles)
    pos_flat = pos[:, :TOP_K].reshape(s * TOP_K)
    tile_e = te[:n_tiles, 0]
    n_used = nu[0, :1]
    xs = _dispatch(pos_flat, h2, jnp.zeros((n_rows, d), F32))
    act = _moe_gate_up(tile_e, n_used, xs, w_gate_up.reshape(N_EXPERTS, d, 2 * D_FF),
                       b_gate_up.reshape(N_EXPERTS, 1, 2 * D_FF))
    ys = _moe_down(tile_e, n_used, act, w_down.reshape(N_EXPERTS, D_FF, d),
                   b_down.reshape(N_EXPERTS, 1, d))
    out = _final(pos_flat, x1, w_top, g2, final_norm_g.reshape(1, d), ys)
    return out.reshape(b, s, d)
```

```python
import functools

import jax
import jax.numpy as jnp
from jax import lax
from jax.experimental import pallas as pl
from jax.experimental.pallas import tpu as pltpu

F32 = jnp.float32
BF16 = jnp.bfloat16

D_MODEL = 2048
M_HEADS = 8
M_DQK = 64
M_DV = 128
M_CONV = 4
F_HEADS = 8
F_DH = 128
N_EXPERTS = 32
TOP_K = 4
D_FF = 2048
SWIGLU_LIMIT = 7.0
SWIGLU_ALPHA = 1.702
EPS = 1e-6
LOG2E = 1.4426950408889634

M_QK_W = M_HEADS * M_DQK
M_V_W = M_HEADS * M_DV
F_W = F_HEADS * F_DH

VMEM_LIMIT = 56 * 1024 * 1024
LANES = 128
SUBLANES = 8

PK_BLOCK = 1024
PK_NBLK = 11
BLK_MQK, BLK_MV, BLK_MO, BLK_FQ, BLK_FK, BLK_FV, BLK_FO, BLK_GA, BLK_GB = 0, 1, 2, 3, 4, 5, 6, 7, 9
W_IN_IGATE = 2 * M_QK_W + M_V_W
W_IN_FGATE = W_IN_IGATE + 2 * M_HEADS + M_V_W + 3 * F_W
PK_SHIFT_GROUPS = ((0, BLK_MQK, BLK_MO), (2 * M_HEADS, BLK_MO, BLK_FO),
                   (2 * M_HEADS + F_HEADS, BLK_FO, PK_NBLK))

NORM_TM = 512
IN_TM = 1024
MLSTM_L = 256
FOX_T = 512
OUT_TM = 256
MOE_TM = 256
MOE_TF = 1024
MOE_TN = 1024
FIN_TM = 256


def _cparams(sem):
    return pltpu.CompilerParams(dimension_semantics=sem, vmem_limit_bytes=VMEM_LIMIT)


def _sigmoid(x):
    return 1.0 / (1.0 + jnp.exp(-x))


def _split_bf16(a):
    c = a * 65537.0
    hi = c - (c - a)
    return hi.astype(BF16), (a - hi).astype(BF16)


def _dot_split(a, w2_ref):
    a_hi, a_lo = _split_bf16(a)
    zz = jnp.dot(a_hi, w2_ref[...], preferred_element_type=F32)
    return (zz[:, :LANES] + zz[:, LANES:]
            + jnp.dot(a_lo, w2_ref[:, :LANES], preferred_element_type=F32))


ADA_TN = 1024


def _ada_kernel(c_ref, w_ref, b_ref, o_ref):
    nrow = w_ref.shape[0] // 8

    def body(r, acc):
        rows = pl.ds(pl.multiple_of(r * 8, 8), 8)
        c = c_ref[rows, :]
        sc = c * _sigmoid(c)
        return acc + sc * w_ref[rows, :]

    acc = lax.fori_loop(0, nrow, body, jnp.zeros((8, w_ref.shape[1]), F32), unroll=8)
    o_ref[...] = jnp.sum(acc, axis=0, keepdims=True) + b_ref[...]


def _ada_mod(c_col, ada_w, ada_b):
    d, n = ada_w.shape
    return pl.pallas_call(
        _ada_kernel,
        out_shape=jax.ShapeDtypeStruct((1, n), F32),
        grid=(n // ADA_TN,),
        in_specs=[
            pl.BlockSpec((d, 1), lambda j: (0, 0)),
            pl.BlockSpec((d, ADA_TN), lambda j: (0, j)),
            pl.BlockSpec((1, ADA_TN), lambda j: (0, j)),
        ],
        out_specs=pl.BlockSpec((1, ADA_TN), lambda j: (0, j)),
        compiler_params=_cparams(("arbitrary",)),
        name="ada_mod",
    )(c_col, ada_w, ada_b)


def _head_rmsnorm(a, g_row, scale):
    outs = []
    for h in range(F_HEADS):
        ah = a[:, h * F_DH:(h + 1) * F_DH]
        ms = jnp.mean(ah * ah, axis=-1, keepdims=True)
        outs.append(ah * lax.rsqrt(ms + EPS) * (g_row[:, h * F_DH:(h + 1) * F_DH] * scale))
    return jnp.concatenate(outs, axis=-1)


def _norm1_kernel(x_ref, n1_ref, sc1_ref, sh1_ref, wg_ref, h_ref, zg_ref):
    x = x_ref[...]
    y = x * lax.rsqrt(jnp.mean(x * x, axis=-1, keepdims=True) + EPS) * n1_ref[...]
    h = y * (1.0 + sc1_ref[...]) + sh1_ref[...]
    h_ref[...] = h.astype(BF16)
    zg_ref[...] = _dot_split(h, wg_ref)


def _norm1(x, n1, sc1, sh1, w_gates):
    s, d = x.shape
    tm = NORM_TM
    row = lambda m: (0, 0)
    return pl.pallas_call(
        _norm1_kernel,
        out_shape=(jax.ShapeDtypeStruct((s, d), BF16), jax.ShapeDtypeStruct((s, LANES), F32)),
        grid=(s // tm,),
        in_specs=[pl.BlockSpec((tm, d), lambda m: (m, 0)), pl.BlockSpec((1, d), row),
                  pl.BlockSpec((1, d), row), pl.BlockSpec((1, d), row),
                  pl.BlockSpec((d, 2 * LANES), row)],
        out_specs=(pl.BlockSpec((tm, d), lambda m: (m, 0)),
                   pl.BlockSpec((tm, LANES), lambda m: (m, 0))),
        compiler_params=_cparams(("arbitrary",)),
        name="norm1",
    )(x, n1, sc1, sh1, w_gates)


def _inproj_kernel(h_ref, wa_ref, wb_ref, qg_ref, kg_ref, z_ref, w_s):
    n = pl.program_id(0)
    m = pl.program_id(1)

    for shift, lo, hi in PK_SHIFT_GROUPS:
        @pl.when(jnp.logical_and(m == 0, jnp.logical_and(n >= lo, n < hi)))
        def _(shift=shift):
            if shift == 0:
                w_s[...] = wa_ref[...].astype(BF16)
            else:
                w_s[...] = jnp.concatenate([wa_ref[:, shift:], wb_ref[:, :shift]],
                                           axis=1).astype(BF16)

    acc = jnp.dot(h_ref[...], w_s[...], preferred_element_type=F32)

    @pl.when(n == BLK_FQ)
    def _():
        z_ref[...] = _head_rmsnorm(acc, qg_ref[...], F_DH ** -0.5 * LOG2E).astype(BF16)

    @pl.when(n == BLK_FK)
    def _():
        z_ref[...] = _head_rmsnorm(acc, kg_ref[...], 1.0).astype(BF16)

    @pl.when(jnp.logical_and(n != BLK_FQ, n != BLK_FK))
    def _():
        z_ref[...] = acc.astype(BF16)


def _inproj(h, w_in, qg, kg):
    s, d = h.shape
    row = lambda n, m: (0, 0)
    lpb = PK_BLOCK // LANES
    return pl.pallas_call(
        _inproj_kernel,
        out_shape=jax.ShapeDtypeStruct((s, PK_NBLK * PK_BLOCK), BF16),
        grid=(PK_NBLK, s // IN_TM),
        in_specs=[
            pl.BlockSpec((IN_TM, d), lambda n, m: (m, 0)),
            pl.BlockSpec((d, PK_BLOCK), lambda n, m: (0, n)),
            pl.BlockSpec((d, LANES), lambda n, m: (0, lpb * (n + 1))),
            pl.BlockSpec((1, PK_BLOCK), row),
            pl.BlockSpec((1, PK_BLOCK), row),
        ],
        out_specs=pl.BlockSpec((IN_TM, PK_BLOCK), lambda n, m: (m, n)),
        scratch_shapes=[pltpu.VMEM((d, PK_BLOCK), BF16)],
        compiler_params=_cparams(("arbitrary", "arbitrary")),
        name="inproj",
    )(h, w_in, w_in, qg, kg)


G_IPRE, G_BM, G_FF = 0, M_HEADS, 2 * M_HEADS
G_ROWS = 3 * M_HEADS
GATE_TM = 512


def _gates_kernel(zg_ref, b_ref, gcol_ref, grow_ref, carry_s):
    t = pl.program_id(0)
    tm = zg_ref.shape[0]

    @pl.when(t == 0)
    def _():
        carry_s[...] = jnp.zeros(carry_s.shape, F32)

    a = zg_ref[...] + b_ref[...]
    lane = lax.broadcasted_iota(jnp.int32, a.shape, 1)
    logf = jnp.minimum(a, 0.0) - jnp.log1p(jnp.exp(-jnp.abs(a)))
    is_f = jnp.logical_and(lane >= G_BM, lane < G_ROWS)
    r = lax.broadcasted_iota(jnp.int32, (tm, tm), 0)
    c = lax.broadcasted_iota(jnp.int32, (tm, tm), 1)
    tri = jnp.where(r >= c, 1.0, 0.0)
    cum = jnp.dot(tri, jnp.where(is_f, logf, 0.0), preferred_element_type=F32,
                  precision=lax.Precision.HIGHEST) + carry_s[0:1, :]
    carry_s[...] = jnp.broadcast_to(cum[tm - 1:tm, :], carry_s.shape)
    g = jnp.where(lane < G_BM, a, cum)
    gcol_ref[...] = g
    grow_ref[...] = g.T[0:G_ROWS, :]


def _gates(zg, bias_row):
    s = zg.shape[0]
    tm = GATE_TM
    return pl.pallas_call(
        _gates_kernel,
        out_shape=(jax.ShapeDtypeStruct((s, LANES), F32),
                   jax.ShapeDtypeStruct((G_ROWS, s), F32)),
        grid=(s // tm,),
        in_specs=[pl.BlockSpec((tm, LANES), lambda t: (t, 0)),
                  pl.BlockSpec((1, LANES), lambda t: (0, 0))],
        out_specs=(pl.BlockSpec((tm, LANES), lambda t: (t, 0)),
                   pl.BlockSpec((G_ROWS, tm), lambda t: (0, t))),
        scratch_shapes=[pltpu.VMEM((8, LANES), F32)],
        compiler_params=_cparams(("arbitrary",)),
        name="gates",
    )(zg, bias_row)


def _mlstm_kernel(qk_ref, v_ref, og_ref, gcol_ref, irow_ref, brow_ref,
                  cw_ref, cb_ref, gn_ref, out_ref, xbuf, c_s, m_s, bend_s):
    c = pl.program_id(0)
    L = qk_ref.shape[0]

    @pl.when(c == 0)
    def _():
        xbuf[0:8, :] = jnp.zeros((8, xbuf.shape[1]), F32)
        c_s[...] = jnp.zeros(c_s.shape, F32)
        m_s[...] = jnp.zeros(m_s.shape, F32)
        bend_s[...] = jnp.zeros(bend_s.shape, F32)

    xbuf[8:8 + L, :] = qk_ref[...].astype(F32)
    y = cb_ref[...] + cw_ref[0:1, :] * xbuf[5:5 + L, :]
    for j in range(1, M_CONV):
        y = y + cw_ref[j:j + 1, :] * xbuf[5 + j:5 + j + L, :]
    xbuf[0:8, :] = xbuf[L:L + 8, :]
    y = y * _sigmoid(y)

    rows = lax.broadcasted_iota(jnp.int32, (L, L), 0)
    cols = lax.broadcasted_iota(jnp.int32, (L, L), 1)
    causal = rows >= cols
    lane = lax.broadcasted_iota(jnp.int32, (L, M_DV), 1)
    ones_blk = jnp.where(lane == 0, 1.0, 0.0).astype(BF16)

    for h in range(M_HEADS):
        q = y[:, h * M_DQK:(h + 1) * M_DQK].astype(BF16)
        kf = y[:, M_QK_W + h * M_DQK:M_QK_W + (h + 1) * M_DQK] * (M_DQK ** -0.5)
        k = kf.astype(BF16)
        vext = jnp.concatenate([v_ref[:, h * M_DV:(h + 1) * M_DV], ones_blk], axis=-1)
        icol = gcol_ref[:, G_IPRE + h:G_IPRE + h + 1]
        bcol = gcol_ref[:, G_BM + h:G_BM + h + 1]
        rowterm = irow_ref[h:h + 1, :] - brow_ref[h:h + 1, :]
        bprev = bend_s[h:h + 1, 0:1]
        mprev = m_s[h:h + 1, 0:1]

        dmat = jnp.where(causal, bcol + rowterm, -jnp.inf)
        inter = (bcol - bprev) + mprev
        m_t = jnp.maximum(inter, jnp.max(dmat, axis=1, keepdims=True))
        w_intra = jnp.exp(dmat - m_t)
        w_inter = jnp.exp(inter - m_t)
        s = lax.dot_general(q, k, (((1,), (1,)), ((), ())), preferred_element_type=F32) * w_intra
        cx = c_s[h]
        nd = (jnp.dot(s.astype(BF16), vext, preferred_element_type=F32)
              + w_inter * jnp.dot(q, cx.astype(BF16), preferred_element_type=F32))
        num = nd[:, :M_DV]
        den = nd[:, M_DV:M_DV + 1]
        hv = num / jnp.maximum(jnp.abs(den), jnp.exp(-m_t))
        ms = jnp.mean(hv * hv, axis=-1, keepdims=True)
        hn = hv * lax.rsqrt(ms + EPS) * gn_ref[:, h * M_DV:(h + 1) * M_DV]
        og = og_ref[:, h * M_DV:(h + 1) * M_DV].astype(F32)
        out_ref[:, h * M_DV:(h + 1) * M_DV] = (hn * _sigmoid(og)).astype(BF16)

        bend = bcol[L - 1:L, :]
        gtot = bend - bprev
        acol = (bend - bcol) + icol
        m_new = jnp.maximum(gtot + mprev, jnp.max(acol, axis=0, keepdims=True))
        wk = jnp.exp(acol - m_new)
        decay = jnp.exp(gtot + mprev - m_new)
        kw = (kf * wk).astype(BF16)
        upd = lax.dot_general(kw, vext, (((0,), (0,)), ((), ())), preferred_element_type=F32)
        c_s[h] = decay * cx + upd
        m_s[h:h + 1, :] = jnp.broadcast_to(m_new, (1, LANES))
        bend_s[h:h + 1, :] = jnp.broadcast_to(bend, (1, LANES))


def _mlstm(z, gcol, grow, conv_w, conv_b, gnorm):
    s = z.shape[0]
    L = MLSTM_L
    const = lambda c: (0, 0)
    return pl.pallas_call(
        _mlstm_kernel,
        out_shape=jax.ShapeDtypeStruct((s, M_V_W), BF16),
        grid=(s // L,),
        in_specs=[
            pl.BlockSpec((L, PK_BLOCK), lambda c: (c, BLK_MQK)),
            pl.BlockSpec((L, PK_BLOCK), lambda c: (c, BLK_MV)),
            pl.BlockSpec((L, PK_BLOCK), lambda c: (c, BLK_MO)),
            pl.BlockSpec((L, LANES), lambda c: (c, 0)),
            pl.BlockSpec((M_HEADS, L), lambda c: (G_IPRE // M_HEADS, c)),
            pl.BlockSpec((M_HEADS, L), lambda c: (G_BM // M_HEADS, c)),
            pl.BlockSpec((M_CONV, 2 * M_QK_W), const),
            pl.BlockSpec((1, 2 * M_QK_W), const),
            pl.BlockSpec((1, M_V_W), const),
        ],
        out_specs=pl.BlockSpec((L, M_V_W), lambda c: (c, 0)),
        scratch_shapes=[
            pltpu.VMEM((L + 8, 2 * M_QK_W), F32),
            pltpu.VMEM((M_HEADS, M_DQK, 2 * M_DV), F32),
            pltpu.VMEM((M_HEADS, LANES), F32),
            pltpu.VMEM((M_HEADS, LANES), F32),
        ],
        compiler_params=_cparams(("arbitrary",)),
        name="mlstm",
    )(z, z, z, gcol, grow, grow, conv_w, conv_b, gnorm)


FOX_KX = 2 * F_DH
FOX_NPIECE = 3
FOX_VT = F_DH + 16
FOX_VT_ONE = F_DH


FOX_MASKED = -1e30


def _fox_kernel(q_ref, k_ref, v_ref, f_ref, og_ref, o_ref, kx_s, vt_s, s0, s1, p0, p1,
                acc_s, m_s):
    i = pl.program_id(1)
    T = q_ref.shape[0]
    S = k_ref.shape[0]

    @pl.when(i == 0)
    def _():
        lane = lax.broadcasted_iota(jnp.int32, (T, F_DH), 1)
        row = lax.broadcasted_iota(jnp.int32, (FOX_VT - F_DH, T), 0)
        ones_rows = jnp.where(row == 0, 1.0, 0.0).astype(BF16)

        def prep(c, carry):
            off = pl.multiple_of(c * T, T)
            fsel = jnp.where(lane == G_FF + pl.program_id(0), f_ref[pl.ds(off, T), :], 0.0)
            rem = -LOG2E * jnp.sum(fsel, axis=1, keepdims=True)
            ext = jnp.zeros((T, F_DH), F32)
            for piece in range(FOX_NPIECE):
                part = rem.astype(BF16).astype(F32)
                ext = jnp.where(lane == piece, part, ext)
                rem = rem - part
            kx_s[pl.ds(off, T), 0:F_DH] = k_ref[pl.ds(off, T), :]
            kx_s[pl.ds(off, T), F_DH:FOX_KX] = ext.astype(BF16)
            vt_s[0:F_DH, pl.ds(off, T)] = v_ref[pl.ds(off, T), :].astype(F32).T.astype(BF16)
            vt_s[F_DH:FOX_VT, pl.ds(off, T)] = ones_rows
            return carry

        lax.fori_loop(0, S // T, prep, 0)

    lane_q = lax.broadcasted_iota(jnp.int32, (T, F_DH), 1)
    qx = jnp.concatenate(
        [q_ref[...], jnp.where(lane_q < FOX_NPIECE, 1.0, 0.0).astype(BF16)], axis=1)

    def score(blk, s_buf, mask):
        off = pl.multiple_of(blk * T, T)
        st = lax.dot_general(kx_s[pl.ds(off, T), :], qx, (((1,), (1,)), ((), ())),
                             preferred_element_type=F32)
        if mask is not None:
            st = jnp.where(mask, st, FOX_MASKED)
        s_buf[...] = st
        return jnp.max(st, axis=0, keepdims=True)

    def soft(s_buf, p_buf, m, bmax):
        m_new = jnp.maximum(m, bmax)
        p_buf[...] = jnp.exp2(s_buf[...] - m_new).astype(BF16)
        return m_new, jnp.exp2(m - m_new)

    def accum(blk, p_buf, alpha):
        vt = vt_s[:, pl.ds(pl.multiple_of(blk * T, T), T)]
        acc_s[...] = alpha * acc_s[...] + jnp.dot(vt, p_buf[...], preferred_element_type=F32)

    keys = lax.broadcasted_iota(jnp.int32, (T, T), 0)
    queries = lax.broadcasted_iota(jnp.int32, (T, T), 1)
    causal = queries >= keys
    odd = i % 2
    acc_s[...] = jnp.zeros(acc_s.shape, F32)
    m_s[...] = jnp.full(m_s.shape, FOX_MASKED, F32)

    @pl.when(odd == 1)
    def _():
        bmax = score(i, s0, causal)
        m_new, alpha = soft(s0, p0, m_s[...], bmax)
        accum(i, p0, alpha)
        m_s[...] = m_new

    def block_of(u):
        return jnp.where(jnp.logical_and(odd == 0, u == 0), i, u - 1 + odd)

    last = i - odd
    first_mask = jnp.logical_or(causal, odd == 1)
    bmax = score(block_of(0), s0, first_mask)
    m, alpha = soft(s0, p0, m_s[...], bmax)
    bmax = score(block_of(jnp.minimum(1, last)), s1, None)

    def body(d, carry):
        m, alpha, bmax = carry
        u = 2 * d
        accum(block_of(u), p0, alpha)
        m, alpha = soft(s1, p1, m, bmax)
        bmax = score(block_of(u + 2), s0, None)
        accum(block_of(u + 1), p1, alpha)
        m, alpha = soft(s0, p0, m, bmax)
        bmax = score(block_of(jnp.minimum(u + 3, last)), s1, None)
        return m, alpha, bmax

    m, alpha, _ = lax.fori_loop(0, last // 2, body, (m, alpha, bmax))
    accum(block_of(last), p0, alpha)
    acc = acc_s[...]
    out_t = acc[0:F_DH, :] / acc[FOX_VT_ONE:FOX_VT_ONE + 1, :]
    o_ref[...] = (out_t.T * _sigmoid(og_ref[...].astype(F32))).astype(BF16)


def _fox(z, fcum):
    s = z.shape[0]
    T = FOX_T
    cpb = PK_BLOCK // F_DH
    return pl.pallas_call(
        _fox_kernel,
        out_shape=jax.ShapeDtypeStruct((s, F_W), BF16),
        grid=(F_HEADS, s // T),
        in_specs=[
            pl.BlockSpec((T, F_DH), lambda h, i: (i, BLK_FQ * cpb + h)),
            pl.BlockSpec((s, F_DH), lambda h, i: (0, BLK_FK * cpb + h)),
            pl.BlockSpec((s, F_DH), lambda h, i: (0, BLK_FV * cpb + h)),
            pl.BlockSpec((s, LANES), lambda h, i: (0, 0)),
            pl.BlockSpec((T, F_DH), lambda h, i: (i, BLK_FO * cpb + h)),
        ],
        out_specs=pl.BlockSpec((T, F_DH), lambda h, i: (i, h)),
        scratch_shapes=[pltpu.VMEM((s, FOX_KX), BF16), pltpu.VMEM((FOX_VT, s), BF16),
                        pltpu.VMEM((T, T), F32), pltpu.VMEM((T, T), F32),
                        pltpu.VMEM((T, T), BF16), pltpu.VMEM((T, T), BF16),
                        pltpu.VMEM((FOX_VT, T), F32), pltpu.VMEM((1, T), F32)],
        compiler_params=_cparams(("arbitrary", "arbitrary")),
        name="fox_attention",
    )(z, z, z, fcum, z)


def _outproj_kernel(hm_ref, hf_ref, ga0_ref, ga1_ref, gb0_ref, gb1_ref, x_ref, wm_ref, wf_ref, wo_ref,
                    g1_ref, n2_ref, sc2_ref, sh2_ref, wr_ref, br_ref,
                    x1_ref, h2_ref, lg_ref):
    ym = jnp.dot(hm_ref[...], wm_ref[...], preferred_element_type=F32)
    yf = jnp.dot(hf_ref[...], wf_ref[...], preferred_element_type=F32)
    ga = jnp.concatenate([ga0_ref[...], ga1_ref[...]], axis=1).astype(F32)
    gb = jnp.concatenate([gb0_ref[...], gb1_ref[...]], axis=1).astype(F32)
    y = _sigmoid(ga) * ym + _sigmoid(gb) * yf
    mix = jnp.dot(y.astype(BF16), wo_ref[...], preferred_element_type=F32)
    x1 = x_ref[...] + g1_ref[...] * mix
    x1_ref[...] = x1
    hn = x1 * lax.rsqrt(jnp.mean(x1 * x1, axis=-1, keepdims=True) + EPS) * n2_ref[...]
    h2 = hn * (1.0 + sc2_ref[...]) + sh2_ref[...]
    h2_ref[...] = h2
    lg_ref[...] = _dot_split(h2, wr_ref) + br_ref[...]


def _outproj(hm, hf, z, x, wm, wf, wo, g1, n2, sc2, sh2, wr, br):
    s, d = x.shape
    tm = OUT_TM
    const = lambda m: (0, 0)
    return pl.pallas_call(
        _outproj_kernel,
        out_shape=(jax.ShapeDtypeStruct((s, d), F32),
                   jax.ShapeDtypeStruct((s, d), F32),
                   jax.ShapeDtypeStruct((s, LANES), F32)),
        grid=(s // tm,),
        in_specs=[
            pl.BlockSpec((tm, M_V_W), lambda m: (m, 0)),
            pl.BlockSpec((tm, F_W), lambda m: (m, 0)),
            pl.BlockSpec((tm, PK_BLOCK), lambda m: (m, BLK_GA)),
            pl.BlockSpec((tm, PK_BLOCK), lambda m: (m, BLK_GA + 1)),
            pl.BlockSpec((tm, PK_BLOCK), lambda m: (m, BLK_GB)),
            pl.BlockSpec((tm, PK_BLOCK), lambda m: (m, BLK_GB + 1)),
            pl.BlockSpec((tm, d), lambda m: (m, 0)),
            pl.BlockSpec((M_V_W, d), const, pipeline_mode=pl.Buffered(1)),
            pl.BlockSpec((F_W, d), const, pipeline_mode=pl.Buffered(1)),
            pl.BlockSpec((d, d), const, pipeline_mode=pl.Buffered(1)),
            pl.BlockSpec((1, d), const),
            pl.BlockSpec((1, d), const),
            pl.BlockSpec((1, d), const),
            pl.BlockSpec((1, d), const),
            pl.BlockSpec((d, 2 * LANES), const),
            pl.BlockSpec((1, LANES), const),
        ],
        out_specs=(pl.BlockSpec((tm, d), lambda m: (m, 0)),
                   pl.BlockSpec((tm, d), lambda m: (m, 0)),
                   pl.BlockSpec((tm, LANES), lambda m: (m, 0))),
        compiler_params=_cparams(("arbitrary",)),
        name="outproj",
    )(hm, hf, z, z, z, z, x, wm, wf, wo, g1, n2, sc2, sh2, wr, br)


def _moe_gu_kernel(te_ref, nu_ref, x_ref, wg_ref, wu_ref, bg_ref, bu_ref, o_ref):
    i = pl.program_id(1)

    @pl.when(i < nu_ref[0])
    def _():
        x = x_ref[...].astype(BF16)
        g = jnp.dot(x, wg_ref[0].astype(BF16), preferred_element_type=F32) + bg_ref[0]
        u = jnp.dot(x, wu_ref[0].astype(BF16), preferred_element_type=F32) + bu_ref[0]
        gate = jnp.minimum(g, SWIGLU_LIMIT)
        up = jnp.clip(u, -SWIGLU_LIMIT, SWIGLU_LIMIT)
        act = (up + 1.0) * gate * _sigmoid(SWIGLU_ALPHA * gate)
        o_ref[...] = act.astype(BF16)

    @pl.when(i >= nu_ref[0])
    def _():
        o_ref[...] = jnp.zeros(o_ref.shape, BF16)


def _moe_gate_up(tile_e, n_used, xs, w_gu, b_gu):
    r, d = xs.shape
    nf = D_FF // MOE_TF
    grid_spec = pltpu.PrefetchScalarGridSpec(
        num_scalar_prefetch=2,
        grid=(nf, r // MOE_TM),
        in_specs=[
            pl.BlockSpec((MOE_TM, d), lambda f, i, te, nu: (jnp.minimum(i, nu[0] - 1), 0)),
            pl.BlockSpec((1, d, MOE_TF), lambda f, i, te, nu: (te[i], 0, f)),
            pl.BlockSpec((1, d, MOE_TF), lambda f, i, te, nu: (te[i], 0, nf + f)),
            pl.BlockSpec((1, 1, MOE_TF), lambda f, i, te, nu: (te[i], 0, f)),
            pl.BlockSpec((1, 1, MOE_TF), lambda f, i, te, nu: (te[i], 0, nf + f)),
        ],
        out_specs=pl.BlockSpec((MOE_TM, MOE_TF), lambda f, i, te, nu: (i, f)),
    )
    return pl.pallas_call(
        _moe_gu_kernel,
        out_shape=jax.ShapeDtypeStruct((r, D_FF), BF16),
        grid_spec=grid_spec,
        compiler_params=_cparams(("arbitrary", "arbitrary")),
        name="moe_gate_up",
    )(tile_e, n_used, xs, w_gu, w_gu, b_gu, b_gu)


def _moe_down_kernel(te_ref, nu_ref, a_ref, wd_ref, bd_ref, o_ref):
    i = pl.program_id(1)

    @pl.when(i < nu_ref[0])
    def _():
        o_ref[...] = (jnp.dot(a_ref[...], wd_ref[0].astype(BF16), preferred_element_type=F32)
                      + bd_ref[0])

    @pl.when(i >= nu_ref[0])
    def _():
        o_ref[...] = jnp.zeros(o_ref.shape, F32)


def _moe_down(tile_e, n_used, act, w_d, b_d):
    r = act.shape[0]
    nn = D_MODEL // MOE_TN
    grid_spec = pltpu.PrefetchScalarGridSpec(
        num_scalar_prefetch=2,
        grid=(nn, r // MOE_TM),
        in_specs=[
            pl.BlockSpec((MOE_TM, D_FF), lambda n, i, te, nu: (jnp.minimum(i, nu[0] - 1), 0)),
            pl.BlockSpec((1, D_FF, MOE_TN), lambda n, i, te, nu: (te[i], 0, n)),
            pl.BlockSpec((1, 1, MOE_TN), lambda n, i, te, nu: (te[i], 0, n)),
        ],
        out_specs=pl.BlockSpec((MOE_TM, MOE_TN), lambda n, i, te, nu: (i, n)),
    )
    return pl.pallas_call(
        _moe_down_kernel,
        out_shape=jax.ShapeDtypeStruct((r, D_MODEL), F32),
        grid_spec=grid_spec,
        compiler_params=_cparams(("arbitrary", "arbitrary")),
        name="moe_down",
    )(tile_e, n_used, act, w_d, b_d)


def _final_kernel(pos_ref, x1_ref, w_ref, g2_ref, gf_ref, y_hbm, o_ref, buf, sem):
    m = pl.program_id(0)
    nm = pl.num_programs(0)
    tm = x1_ref.shape[0]

    def issue(step, slot):
        base = step * (tm * TOP_K)

        def body(t, carry):
            for k in range(TOP_K):
                r = pos_ref[base + t * TOP_K + k]
                pltpu.make_async_copy(y_hbm.at[pl.ds(r, 1), :],
                                      buf.at[slot, k, pl.ds(t, 1), :], sem.at[slot]).start()
            return carry

        lax.fori_loop(0, tm, body, 0)

    @pl.when(m == 0)
    def _():
        issue(0, 0)

    @pl.when(m + 1 < nm)
    def _():
        issue(m + 1, (m + 1) % 2)

    slot = m % 2
    for k in range(TOP_K):
        pltpu.make_async_copy(y_hbm.at[pl.ds(0, tm), :], buf.at[slot, k], sem.at[slot]).wait()
    w = w_ref[...]
    moe = w[:, 0:1] * buf[slot, 0]
    for k in range(1, TOP_K):
        moe = moe + w[:, k:k + 1] * buf[slot, k]
    x2 = x1_ref[...] + g2_ref[...] * moe
    o_ref[...] = x2 * lax.rsqrt(jnp.mean(x2 * x2, axis=-1, keepdims=True) + EPS) * gf_ref[...]


def _final(pos_flat, x1, w_top, g2, gf, ys):
    s, d = x1.shape
    tm = FIN_TM
    const = lambda m, pos: (0, 0)
    grid_spec = pltpu.PrefetchScalarGridSpec(
        num_scalar_prefetch=1,
        grid=(s // tm,),
        in_specs=[
            pl.BlockSpec((tm, d), lambda m, pos: (m, 0)),
            pl.BlockSpec((tm, LANES), lambda m, pos: (m, 0)),
            pl.BlockSpec((1, d), const),
            pl.BlockSpec((1, d), const),
            pl.BlockSpec(memory_space=pl.ANY),
        ],
        out_specs=pl.BlockSpec((tm, d), lambda m, pos: (m, 0)),
        scratch_shapes=[pltpu.VMEM((2, TOP_K, tm, d), F32), pltpu.SemaphoreType.DMA((2,))],
    )
    return pl.pallas_call(
        _final_kernel,
        out_shape=jax.ShapeDtypeStruct((s, d), F32),
        grid_spec=grid_spec,
        compiler_params=_cparams(("arbitrary",)),
        name="final_combine",
    )(pos_flat, x1, w_top, g2, gf, ys)


ROUTE_TM = 512
META_NUSED, META_PAD0, META_PADN = 0, 1, 2


def _route_kernel(lg_ref, pos_ref, w_ref, te_ref, nu_ref, cnt_s, pstart_s, carry_s):
    p = pl.program_id(0)
    m = pl.program_id(1)
    tm = lg_ref.shape[0]
    lane = lax.broadcasted_iota(jnp.int32, (tm, LANES), 1)
    lanef = lane.astype(F32)
    l = jnp.where(lane < N_EXPERTS, lg_ref[...], -jnp.inf)
    sel = jnp.zeros((tm, LANES), F32)
    vals, hots = [], []
    for _ in range(TOP_K):
        mx = jnp.max(l, axis=1, keepdims=True)
        idx = jnp.min(jnp.where(l == mx, lanef, float(LANES)), axis=1, keepdims=True)
        hot = lanef == idx
        vals.append(mx)
        hots.append(hot)
        l = jnp.where(hot, -jnp.inf, l)
        sel = sel + jnp.where(hot, 1.0, 0.0)
    colsum = jnp.sum(sel, axis=0, keepdims=True)

    @pl.when(p == 0)
    def _():
        @pl.when(m == 0)
        def _():
            cnt_s[...] = jnp.zeros(cnt_s.shape, F32)

        cnt_s[...] = cnt_s[...] + colsum

    @pl.when(jnp.logical_and(p == 1, m == 0))
    def _():
        cnt = cnt_s[...]
        padded = jnp.floor((cnt + (MOE_TM - 1.0)) * (1.0 / MOE_TM)) * MOE_TM
        lane8 = lax.broadcasted_iota(jnp.int32, cnt.shape, 1)
        pend = padded
        d = 1
        while d < LANES:
            pend = pend + jnp.where(lane8 >= d, pltpu.roll(pend, d, axis=1), 0.0)
            d *= 2
        pstart_s[...] = pend - padded
        carry_s[...] = jnp.zeros(carry_s.shape, F32)
        total = pend[:, N_EXPERTS - 1:N_EXPERTS]
        nt = te_ref.shape[0]
        tstart = lax.broadcasted_iota(jnp.int32, (nt, LANES), 0).astype(F32) * MOE_TM
        lane_t = lax.broadcasted_iota(jnp.int32, (nt, LANES), 1)
        done = jnp.where(lane_t < N_EXPERTS, jnp.where(tstart >= pend[0:1, :], 1.0, 0.0), 0.0)
        te = jnp.sum(done, axis=1, keepdims=True)
        last = jnp.sum(jnp.where(lane8 < N_EXPERTS,
                                 jnp.where(pend <= total - 1.0, 1.0, 0.0), 0.0),
                       axis=1, keepdims=True)
        te_ref[...] = jnp.minimum(te, last[0:1, :]).astype(jnp.int32)
        mrow = lax.broadcasted_iota(jnp.int32, cnt.shape, 0)
        meta = jnp.where(mrow == META_NUSED, jnp.broadcast_to(total * (1.0 / MOE_TM), cnt.shape),
                         jnp.where(mrow == META_PAD0, pend - padded + cnt, padded - cnt))
        nu_ref[...] = meta.astype(jnp.int32)

    @pl.when(p == 1)
    def _():
        r = lax.broadcasted_iota(jnp.int32, (tm, tm), 0)
        c = lax.broadcasted_iota(jnp.int32, (tm, tm), 1)
        tri = jnp.where(r > c, 1.0, 0.0).astype(BF16)
        rowbase = (jnp.dot(tri, sel.astype(BF16), preferred_element_type=F32)
                   + carry_s[0:1, :] + pstart_s[0:1, :])
        carry_s[...] = carry_s[...] + colsum
        den = jnp.exp(vals[0] - vals[0])
        for k in range(1, TOP_K):
            den = den + jnp.exp(vals[k] - vals[0])
        posf = jnp.zeros((tm, LANES), F32)
        wf = jnp.zeros((tm, LANES), F32)
        for k in range(TOP_K):
            pk = jnp.sum(jnp.where(hots[k], rowbase, 0.0), axis=1, keepdims=True)
            wk = jnp.exp(vals[k] - vals[0]) / den
            posf = jnp.where(lane == k, pk, posf)
            wf = jnp.where(lane == k, wk, wf)
        pos_ref[...] = posf.astype(jnp.int32)
        w_ref[...] = wf


def _route(lg, n_tiles):
    s = lg.shape[0]
    tm = ROUTE_TM
    nt_pad = -(-n_tiles // 8) * 8
    const = lambda p, m: (0, 0)
    return pl.pallas_call(
        _route_kernel,
        out_shape=(jax.ShapeDtypeStruct((s, LANES), jnp.int32),
                   jax.ShapeDtypeStruct((s, LANES), F32),
                   jax.ShapeDtypeStruct((nt_pad, 1), jnp.int32),
                   jax.ShapeDtypeStruct((8, LANES), jnp.int32)),
        grid=(2, s // tm),
        in_specs=[pl.BlockSpec((tm, LANES), lambda p, m: (m, 0))],
        out_specs=(pl.BlockSpec((tm, LANES), lambda p, m: (m * p, 0)),
                   pl.BlockSpec((tm, LANES), lambda p, m: (m * p, 0)),
                   pl.BlockSpec((nt_pad, 1), const),
                   pl.BlockSpec((8, LANES), const)),
        scratch_shapes=[pltpu.VMEM((8, LANES), F32), pltpu.VMEM((8, LANES), F32),
                        pltpu.VMEM((8, LANES), F32)],
        compiler_params=_cparams(("arbitrary", "arbitrary")),
        name="route",
    )(lg)


DISP_TM = 256


ZERO_ROWS = MOE_TM // 2


def _dispatch_kernel(pos_ref, pad0_ref, padn_ref, nu_ref, h_ref, xs_hbm, zero_s, sem, zsem):
    m = pl.program_id(0)
    tm = h_ref.shape[0]
    base = m * (tm * TOP_K)
    n_tiles = xs_hbm.shape[0] // MOE_TM

    def pad_copies(e, act):
        start = pad0_ref[e]
        n = padn_ref[e]
        head = jnp.minimum(n, (-start) & (SUBLANES - 1))

        def one_row(r, c):
            act(pltpu.make_async_copy(zero_s.at[pl.ds(0, 1), :],
                                      xs_hbm.at[pl.ds(start + r, 1), :], zsem))
            return c

        lax.fori_loop(0, head, one_row, 0)
        start = start + head
        n = n - head
        b = ZERO_ROWS
        while b >= SUBLANES:
            @pl.when((n & b) != 0)
            def _(b=b):
                off = pl.multiple_of(start + (n & ~(2 * b - 1)), SUBLANES)
                act(pltpu.make_async_copy(zero_s.at[pl.ds(0, b), :],
                                          xs_hbm.at[pl.ds(off, b), :], zsem))
            b //= 2

    def tail_copies(j, act):
        for half in range(MOE_TM // ZERO_ROWS):
            off = pl.multiple_of(j * MOE_TM + half * ZERO_ROWS, ZERO_ROWS)
            act(pltpu.make_async_copy(zero_s, xs_hbm.at[pl.ds(off, ZERO_ROWS), :], zsem))

    @pl.when(m == 0)
    def _():
        zero_s[...] = jnp.zeros(zero_s.shape, F32)
        for act in (lambda cp: cp.start(), lambda cp: cp.wait()):
            lax.fori_loop(0, N_EXPERTS, lambda e, c, act=act: (pad_copies(e, act), c)[1], 0)
            lax.fori_loop(nu_ref[0], n_tiles, lambda j, c, act=act: (tail_copies(j, act), c)[1], 0)

    def body(t, carry):
        for k in range(TOP_K):
            r = pos_ref[base + t * TOP_K + k]
            pltpu.make_async_copy(h_ref.at[pl.ds(t, 1), :], xs_hbm.at[pl.ds(r, 1), :], sem).start()
        return carry

    lax.fori_loop(0, tm, body, 0)
    for _ in range(TOP_K):
        pltpu.make_async_copy(h_ref, xs_hbm.at[pl.ds(0, tm), :], sem).wait()


def _dispatch(pos_flat, pad0, padn, n_used, h2, n_rows):
    s, d = h2.shape
    tm = DISP_TM
    grid_spec = pltpu.PrefetchScalarGridSpec(
        num_scalar_prefetch=4,
        grid=(s // tm,),
        in_specs=[pl.BlockSpec((tm, d), lambda m, *_: (m, 0))],
        out_specs=pl.BlockSpec(memory_space=pl.ANY),
        scratch_shapes=[pltpu.VMEM((ZERO_ROWS, d), F32), pltpu.SemaphoreType.DMA(()),
                        pltpu.SemaphoreType.DMA(())],
    )
    return pl.pallas_call(
        _dispatch_kernel,
        out_shape=jax.ShapeDtypeStruct((n_rows, d), F32),
        grid_spec=grid_spec,
        compiler_params=_cparams(("arbitrary",)),
        name="dispatch",
    )(pos_flat, pad0, padn, n_used, h2)


def _gate_weights(w_in):
    return jnp.concatenate([w_in[:, W_IN_IGATE:W_IN_IGATE + 2 * M_HEADS],
                            w_in[:, W_IN_FGATE:W_IN_FGATE + F_HEADS],
                            jnp.zeros((w_in.shape[0], LANES - G_ROWS), F32)], axis=1)


def kernel(x, c, ada_w, ada_b, norm1_g, w_in, m_conv_w, m_conv_b, m_i_bias, m_f_bias,
           m_out_norm_g, f_f_bias, f_q_norm_g, f_k_norm_g, w_branch_m, w_branch_f, w_out,
           norm2_g, w_router, b_router, w_gate_up, b_gate_up, w_down, b_down, final_norm_g):
    b, s, d = x.shape
    assert b == 1 and ada_w.shape[0] == 1
    x2d = x.reshape(s, d)

    mod = _ada_mod(c.reshape(d, 1), ada_w.reshape(d, 6 * d), ada_b.reshape(1, 6 * d))
    sh1, sc1, g1, sh2, sc2, g2 = [mod[:, i * d:(i + 1) * d] for i in range(6)]

    w_in2d = w_in.reshape(d, -1)
    wg2 = jnp.concatenate(_split_bf16(_gate_weights(w_in2d)), axis=1)
    h1, zg = _norm1(x2d, norm1_g.reshape(1, d), sc1, sh1, wg2)
    z = _inproj(h1, w_in2d, f_q_norm_g.reshape(1, F_W), f_k_norm_g.reshape(1, F_W))

    gate_bias = jnp.concatenate([m_i_bias.reshape(1, M_HEADS), m_f_bias.reshape(1, M_HEADS),
                                 f_f_bias.reshape(1, F_HEADS),
                                 jnp.zeros((1, LANES - G_ROWS), F32)], axis=1)
    gcol, grow = _gates(zg, gate_bias)

    hm = _mlstm(z, gcol, grow, m_conv_w.reshape(M_CONV, 2 * M_QK_W),
                m_conv_b.reshape(1, 2 * M_QK_W), m_out_norm_g.reshape(1, M_V_W))
    hf = _fox(z, gcol)

    wr = jnp.concatenate([w_router.reshape(d, N_EXPERTS),
                          jnp.zeros((d, LANES - N_EXPERTS), F32)], axis=1)
    br = jnp.concatenate([b_router.reshape(1, N_EXPERTS),
                          jnp.zeros((1, LANES - N_EXPERTS), F32)], axis=1)
    x1, h2, lg = _outproj(hm, hf, z, x2d,
                          w_branch_m.reshape(M_V_W, d).astype(BF16),
                          w_branch_f.reshape(F_W, d).astype(BF16),
                          w_out.reshape(d, d).astype(BF16),
                          g1, norm2_g.reshape(1, d), sc2, sh2,
                          jnp.concatenate(_split_bf16(wr), axis=1), br)

    n_rows = s * TOP_K + N_EXPERTS * MOE_TM
    n_tiles = n_rows // MOE_TM
    pos, w_top, te, meta = _route(lg, n_tiles)
    pos_flat = pos[:, :TOP_K].reshape(s * TOP_K)
    tile_e = te[:n_tiles, 0]
    n_used = meta[META_NUSED, :1]
    xs = _dispatch(pos_flat, meta[META_PAD0, :N_EXPERTS], meta[META_PADN, :N_EXPERTS], n_used,
                   h2, n_rows)
    act = _moe_gate_up(tile_e, n_used, xs, w_gate_up.reshape(N_EXPERTS, d, 2 * D_FF),
                       b_gate_up.reshape(N_EXPERTS, 1, 2 * D_FF))
    ys = _moe_down(tile_e, n_used, act, w_down.reshape(N_EXPERTS, D_FF, d),
                   b_down.reshape(N_EXPERTS, 1, d))
    out = _final(pos_flat, x1, w_top, g2, final_norm_g.reshape(1, d), ys)
    return out.reshape(b, s, d)
```

```python
import functools

import jax
import jax.numpy as jnp
from jax import lax
from jax.experimental import pallas as pl
from jax.experimental.pallas import tpu as pltpu

F32 = jnp.float32
BF16 = jnp.bfloat16

D_MODEL = 2048
M_HEADS = 8
M_DQK = 64
M_DV = 128
M_CONV = 4
F_HEADS = 8
F_DH = 128
N_EXPERTS = 32
TOP_K = 4
D_FF = 2048
SWIGLU_LIMIT = 7.0
SWIGLU_ALPHA = 1.702
EPS = 1e-6
LOG2E = 1.4426950408889634

M_QK_W = M_HEADS * M_DQK
M_V_W = M_HEADS * M_DV
F_W = F_HEADS * F_DH

VMEM_LIMIT = 56 * 1024 * 1024
LANES = 128
SUBLANES = 8

PK_BLOCK = 1024
PK_NBLK = 11
BLK_MQK, BLK_MV, BLK_MO, BLK_FQ, BLK_FK, BLK_FV, BLK_FO, BLK_GA, BLK_GB = 0, 1, 2, 3, 4, 5, 6, 7, 9
W_IN_IGATE = 2 * M_QK_W + M_V_W
W_IN_FGATE = W_IN_IGATE + 2 * M_HEADS + M_V_W + 3 * F_W
PK_SHIFT_GROUPS = ((0, BLK_MQK, BLK_MO), (2 * M_HEADS, BLK_MO, BLK_FO),
                   (2 * M_HEADS + F_HEADS, BLK_FO, PK_NBLK))

NORM_TM = 512
IN_TM = 1024
MLSTM_L = 256
FOX_T = 512
OUT_TM = 256
MOE_TM = 256
MOE_TF = 1024
MOE_TN = 1024
FIN_TM = 256
DMA_ISSUE_UNROLL = 4


def _cparams(sem):
    return pltpu.CompilerParams(dimension_semantics=sem, vmem_limit_bytes=VMEM_LIMIT)


def _sigmoid(x):
    return 1.0 / (1.0 + jnp.exp(-x))


def _split_bf16(a):
    c = a * 65537.0
    hi = c - (c - a)
    return hi.astype(BF16), (a - hi).astype(BF16)


def _dot_split(a, w2_ref):
    a_hi, a_lo = _split_bf16(a)
    zz = jnp.dot(a_hi, w2_ref[...], preferred_element_type=F32)
    return (zz[:, :LANES] + zz[:, LANES:]
            + jnp.dot(a_lo, w2_ref[:, :LANES], preferred_element_type=F32))


ADA_TN = 1024


def _ada_kernel(c_ref, w_ref, b_ref, o_ref):
    nrow = w_ref.shape[0] // 8

    def body(r, acc):
        rows = pl.ds(pl.multiple_of(r * 8, 8), 8)
        c = c_ref[rows, :]
        sc = c * _sigmoid(c)
        return acc + sc * w_ref[rows, :]

    acc = lax.fori_loop(0, nrow, body, jnp.zeros((8, w_ref.shape[1]), F32), unroll=8)
    o_ref[...] = jnp.sum(acc, axis=0, keepdims=True) + b_ref[...]


def _ada_mod(c_col, ada_w, ada_b):
    d, n = ada_w.shape
    return pl.pallas_call(
        _ada_kernel,
        out_shape=jax.ShapeDtypeStruct((1, n), F32),
        grid=(n // ADA_TN,),
        in_specs=[
            pl.BlockSpec((d, 1), lambda j: (0, 0)),
            pl.BlockSpec((d, ADA_TN), lambda j: (0, j)),
            pl.BlockSpec((1, ADA_TN), lambda j: (0, j)),
        ],
        out_specs=pl.BlockSpec((1, ADA_TN), lambda j: (0, j)),
        compiler_params=_cparams(("arbitrary",)),
        name="ada_mod",
    )(c_col, ada_w, ada_b)


def _head_rmsnorm(a, g_row, scale):
    outs = []
    for h in range(F_HEADS):
        ah = a[:, h * F_DH:(h + 1) * F_DH]
        ms = jnp.mean(ah * ah, axis=-1, keepdims=True)
        outs.append(ah * lax.rsqrt(ms + EPS) * (g_row[:, h * F_DH:(h + 1) * F_DH] * scale))
    return jnp.concatenate(outs, axis=-1)


def _norm1_kernel(x_ref, n1_ref, sc1_ref, sh1_ref, wg_ref, h_ref, zg_ref):
    x = x_ref[...]
    y = x * lax.rsqrt(jnp.mean(x * x, axis=-1, keepdims=True) + EPS) * n1_ref[...]
    h = y * (1.0 + sc1_ref[...]) + sh1_ref[...]
    h_ref[...] = h.astype(BF16)
    zg_ref[...] = _dot_split(h, wg_ref)


def _norm1(x, n1, sc1, sh1, w_gates):
    s, d = x.shape
    tm = NORM_TM
    row = lambda m: (0, 0)
    return pl.pallas_call(
        _norm1_kernel,
        out_shape=(jax.ShapeDtypeStruct((s, d), BF16), jax.ShapeDtypeStruct((s, LANES), F32)),
        grid=(s // tm,),
        in_specs=[pl.BlockSpec((tm, d), lambda m: (m, 0)), pl.BlockSpec((1, d), row),
                  pl.BlockSpec((1, d), row), pl.BlockSpec((1, d), row),
                  pl.BlockSpec((d, 2 * LANES), row)],
        out_specs=(pl.BlockSpec((tm, d), lambda m: (m, 0)),
                   pl.BlockSpec((tm, LANES), lambda m: (m, 0))),
        compiler_params=_cparams(("arbitrary",)),
        name="norm1",
    )(x, n1, sc1, sh1, w_gates)


def _inproj_kernel(h_ref, wa_ref, wb_ref, qg_ref, kg_ref, z_ref, w_s):
    n = pl.program_id(0)
    m = pl.program_id(1)

    for shift, lo, hi in PK_SHIFT_GROUPS:
        @pl.when(jnp.logical_and(m == 0, jnp.logical_and(n >= lo, n < hi)))
        def _(shift=shift):
            if shift == 0:
                w_s[...] = wa_ref[...].astype(BF16)
            else:
                w_s[...] = jnp.concatenate([wa_ref[:, shift:], wb_ref[:, :shift]],
                                           axis=1).astype(BF16)

    acc = jnp.dot(h_ref[...], w_s[...], preferred_element_type=F32)

    @pl.when(n == BLK_FQ)
    def _():
        z_ref[...] = _head_rmsnorm(acc, qg_ref[...], F_DH ** -0.5 * LOG2E).astype(BF16)

    @pl.when(n == BLK_FK)
    def _():
        z_ref[...] = _head_rmsnorm(acc, kg_ref[...], 1.0).astype(BF16)

    @pl.when(jnp.logical_and(n != BLK_FQ, n != BLK_FK))
    def _():
        z_ref[...] = acc.astype(BF16)


def _inproj(h, w_in, qg, kg):
    s, d = h.shape
    row = lambda n, m: (0, 0)
    lpb = PK_BLOCK // LANES
    return pl.pallas_call(
        _inproj_kernel,
        out_shape=jax.ShapeDtypeStruct((s, PK_NBLK * PK_BLOCK), BF16),
        grid=(PK_NBLK, s // IN_TM),
        in_specs=[
            pl.BlockSpec((IN_TM, d), lambda n, m: (m, 0)),
            pl.BlockSpec((d, PK_BLOCK), lambda n, m: (0, n)),
            pl.BlockSpec((d, LANES), lambda n, m: (0, lpb * (n + 1))),
            pl.BlockSpec((1, PK_BLOCK), row),
            pl.BlockSpec((1, PK_BLOCK), row),
        ],
        out_specs=pl.BlockSpec((IN_TM, PK_BLOCK), lambda n, m: (m, n)),
        scratch_shapes=[pltpu.VMEM((d, PK_BLOCK), BF16)],
        compiler_params=_cparams(("arbitrary", "arbitrary")),
        name="inproj",
    )(h, w_in, w_in, qg, kg)


G_IPRE, G_BM, G_FF = 0, M_HEADS, 2 * M_HEADS
G_ROWS = 3 * M_HEADS
GATE_TM = 512


def _gates_kernel(zg_ref, b_ref, gcol_ref, grow_ref, carry_s):
    t = pl.program_id(0)
    tm = zg_ref.shape[0]

    @pl.when(t == 0)
    def _():
        carry_s[...] = jnp.zeros(carry_s.shape, F32)

    a = zg_ref[...] + b_ref[...]
    lane = lax.broadcasted_iota(jnp.int32, a.shape, 1)
    logf = jnp.minimum(a, 0.0) - jnp.log1p(jnp.exp(-jnp.abs(a)))
    is_f = jnp.logical_and(lane >= G_BM, lane < G_ROWS)
    r = lax.broadcasted_iota(jnp.int32, (tm, tm), 0)
    c = lax.broadcasted_iota(jnp.int32, (tm, tm), 1)
    tri = jnp.where(r >= c, 1.0, 0.0)
    cum = jnp.dot(tri, jnp.where(is_f, logf, 0.0), preferred_element_type=F32,
                  precision=lax.Precision.HIGHEST) + carry_s[0:1, :]
    carry_s[...] = jnp.broadcast_to(cum[tm - 1:tm, :], carry_s.shape)
    g = jnp.where(lane < G_BM, a, cum)
    gcol_ref[...] = g
    grow_ref[...] = g.T[0:G_ROWS, :]


def _gates(zg, bias_row):
    s = zg.shape[0]
    tm = GATE_TM
    return pl.pallas_call(
        _gates_kernel,
        out_shape=(jax.ShapeDtypeStruct((s, LANES), F32),
                   jax.ShapeDtypeStruct((G_ROWS, s), F32)),
        grid=(s // tm,),
        in_specs=[pl.BlockSpec((tm, LANES), lambda t: (t, 0)),
                  pl.BlockSpec((1, LANES), lambda t: (0, 0))],
        out_specs=(pl.BlockSpec((tm, LANES), lambda t: (t, 0)),
                   pl.BlockSpec((G_ROWS, tm), lambda t: (0, t))),
        scratch_shapes=[pltpu.VMEM((8, LANES), F32)],
        compiler_params=_cparams(("arbitrary",)),
        name="gates",
    )(zg, bias_row)


def _mlstm_kernel(qk_ref, v_ref, og_ref, gcol_ref, irow_ref, brow_ref,
                  cw_ref, cb_ref, gn_ref, out_ref, xbuf, c_s, m_s, bend_s):
    c = pl.program_id(0)
    L = qk_ref.shape[0]

    @pl.when(c == 0)
    def _():
        xbuf[0:8, :] = jnp.zeros((8, xbuf.shape[1]), F32)
        c_s[...] = jnp.zeros(c_s.shape, F32)
        m_s[...] = jnp.zeros(m_s.shape, F32)
        bend_s[...] = jnp.zeros(bend_s.shape, F32)

    xbuf[8:8 + L, :] = qk_ref[...].astype(F32)
    y = cb_ref[...] + cw_ref[0:1, :] * xbuf[5:5 + L, :]
    for j in range(1, M_CONV):
        y = y + cw_ref[j:j + 1, :] * xbuf[5 + j:5 + j + L, :]
    xbuf[0:8, :] = xbuf[L:L + 8, :]
    y = y * _sigmoid(y)

    rows = lax.broadcasted_iota(jnp.int32, (L, L), 0)
    cols = lax.broadcasted_iota(jnp.int32, (L, L), 1)
    causal = rows >= cols
    lane = lax.broadcasted_iota(jnp.int32, (L, M_DV), 1)
    ones_blk = jnp.where(lane == 0, 1.0, 0.0).astype(BF16)

    for h in range(M_HEADS):
        q = y[:, h * M_DQK:(h + 1) * M_DQK].astype(BF16)
        kf = y[:, M_QK_W + h * M_DQK:M_QK_W + (h + 1) * M_DQK] * (M_DQK ** -0.5)
        k = kf.astype(BF16)
        vext = jnp.concatenate([v_ref[:, h * M_DV:(h + 1) * M_DV], ones_blk], axis=-1)
        icol = gcol_ref[:, G_IPRE + h:G_IPRE + h + 1]
        bcol = gcol_ref[:, G_BM + h:G_BM + h + 1]
        rowterm = irow_ref[h:h + 1, :] - brow_ref[h:h + 1, :]
        bprev = bend_s[h:h + 1, 0:1]
        mprev = m_s[h:h + 1, 0:1]

        dmat = jnp.where(causal, bcol + rowterm, -jnp.inf)
        inter = (bcol - bprev) + mprev
        m_t = jnp.maximum(inter, jnp.max(dmat, axis=1, keepdims=True))
        w_intra = jnp.exp(dmat - m_t)
        w_inter = jnp.exp(inter - m_t)
        s = lax.dot_general(q, k, (((1,), (1,)), ((), ())), preferred_element_type=F32) * w_intra
        cx = c_s[h]
        nd = (jnp.dot(s.astype(BF16), vext, preferred_element_type=F32)
              + w_inter * jnp.dot(q, cx.astype(BF16), preferred_element_type=F32))
        num = nd[:, :M_DV]
        den = nd[:, M_DV:M_DV + 1]
        hv = num / jnp.maximum(jnp.abs(den), jnp.exp(-m_t))
        ms = jnp.mean(hv * hv, axis=-1, keepdims=True)
        hn = hv * lax.rsqrt(ms + EPS) * gn_ref[:, h * M_DV:(h + 1) * M_DV]
        og = og_ref[:, h * M_DV:(h + 1) * M_DV].astype(F32)
        out_ref[:, h * M_DV:(h + 1) * M_DV] = (hn * _sigmoid(og)).astype(BF16)

        bend = bcol[L - 1:L, :]
        gtot = bend - bprev
        acol = (bend - bcol) + icol
        m_new = jnp.maximum(gtot + mprev, jnp.max(acol, axis=0, keepdims=True))
        wk = jnp.exp(acol - m_new)
        decay = jnp.exp(gtot + mprev - m_new)
        kw = (kf * wk).astype(BF16)
        upd = lax.dot_general(kw, vext, (((0,), (0,)), ((), ())), preferred_element_type=F32)
        c_s[h] = decay * cx + upd
        m_s[h:h + 1, :] = jnp.broadcast_to(m_new, (1, LANES))
        bend_s[h:h + 1, :] = jnp.broadcast_to(bend, (1, LANES))


def _mlstm(z, gcol, grow, conv_w, conv_b, gnorm):
    s = z.shape[0]
    L = MLSTM_L
    const = lambda c: (0, 0)
    return pl.pallas_call(
        _mlstm_kernel,
        out_shape=jax.ShapeDtypeStruct((s, M_V_W), BF16),
        grid=(s // L,),
        in_specs=[
            pl.BlockSpec((L, PK_BLOCK), lambda c: (c, BLK_MQK)),
            pl.BlockSpec((L, PK_BLOCK), lambda c: (c, BLK_MV)),
            pl.BlockSpec((L, PK_BLOCK), lambda c: (c, BLK_MO)),
            pl.BlockSpec((L, LANES), lambda c: (c, 0)),
            pl.BlockSpec((M_HEADS, L), lambda c: (G_IPRE // M_HEADS, c)),
            pl.BlockSpec((M_HEADS, L), lambda c: (G_BM // M_HEADS, c)),
            pl.BlockSpec((M_CONV, 2 * M_QK_W), const),
            pl.BlockSpec((1, 2 * M_QK_W), const),
            pl.BlockSpec((1, M_V_W), const),
        ],
        out_specs=pl.BlockSpec((L, M_V_W), lambda c: (c, 0)),
        scratch_shapes=[
            pltpu.VMEM((L + 8, 2 * M_QK_W), F32),
            pltpu.VMEM((M_HEADS, M_DQK, 2 * M_DV), F32),
            pltpu.VMEM((M_HEADS, LANES), F32),
            pltpu.VMEM((M_HEADS, LANES), F32),
        ],
        compiler_params=_cparams(("arbitrary",)),
        name="mlstm",
    )(z, z, z, gcol, grow, grow, conv_w, conv_b, gnorm)


FOX_KX = 2 * F_DH
FOX_NPIECE = 3
FOX_VT = F_DH + 16
FOX_VT_ONE = F_DH


FOX_MASKED = -1e30


def _fox_kernel(q_ref, k_ref, v_ref, f_ref, og_ref, o_ref, kx_s, vt_s, s0, s1, p0, p1,
                acc_s, m_s):
    i = pl.program_id(1)
    T = q_ref.shape[0]
    S = k_ref.shape[0]

    @pl.when(i == 0)
    def _():
        lane = lax.broadcasted_iota(jnp.int32, (T, F_DH), 1)
        row = lax.broadcasted_iota(jnp.int32, (FOX_VT - F_DH, T), 0)
        ones_rows = jnp.where(row == 0, 1.0, 0.0).astype(BF16)

        def prep(c, carry):
            off = pl.multiple_of(c * T, T)
            fsel = jnp.where(lane == G_FF + pl.program_id(0), f_ref[pl.ds(off, T), :], 0.0)
            rem = -LOG2E * jnp.sum(fsel, axis=1, keepdims=True)
            ext = jnp.zeros((T, F_DH), F32)
            for piece in range(FOX_NPIECE):
                part = rem.astype(BF16).astype(F32)
                ext = jnp.where(lane == piece, part, ext)
                rem = rem - part
            kx_s[pl.ds(off, T), 0:F_DH] = k_ref[pl.ds(off, T), :]
            kx_s[pl.ds(off, T), F_DH:FOX_KX] = ext.astype(BF16)
            vt_s[0:F_DH, pl.ds(off, T)] = v_ref[pl.ds(off, T), :].astype(F32).T.astype(BF16)
            vt_s[F_DH:FOX_VT, pl.ds(off, T)] = ones_rows
            return carry

        lax.fori_loop(0, S // T, prep, 0)

    lane_q = lax.broadcasted_iota(jnp.int32, (T, F_DH), 1)
    qx = jnp.concatenate(
        [q_ref[...], jnp.where(lane_q < FOX_NPIECE, 1.0, 0.0).astype(BF16)], axis=1)

    def score(blk, s_buf, mask):
        off = pl.multiple_of(blk * T, T)
        st = lax.dot_general(kx_s[pl.ds(off, T), :], qx, (((1,), (1,)), ((), ())),
                             preferred_element_type=F32)
        if mask is not None:
            st = jnp.where(mask, st, FOX_MASKED)
        s_buf[...] = st
        return jnp.max(st, axis=0, keepdims=True)

    def soft(s_buf, p_buf, m, bmax):
        m_new = jnp.maximum(m, bmax)
        p_buf[...] = jnp.exp2(s_buf[...] - m_new).astype(BF16)
        return m_new, jnp.exp2(m - m_new)

    def accum(blk, p_buf, alpha):
        vt = vt_s[:, pl.ds(pl.multiple_of(blk * T, T), T)]
        acc_s[...] = alpha * acc_s[...] + jnp.dot(vt, p_buf[...], preferred_element_type=F32)

    keys = lax.broadcasted_iota(jnp.int32, (T, T), 0)
    queries = lax.broadcasted_iota(jnp.int32, (T, T), 1)
    causal = queries >= keys
    odd = i % 2
    acc_s[...] = jnp.zeros(acc_s.shape, F32)
    m_s[...] = jnp.full(m_s.shape, FOX_MASKED, F32)

    @pl.when(odd == 1)
    def _():
        bmax = score(i, s0, causal)
        m_new, alpha = soft(s0, p0, m_s[...], bmax)
        accum(i, p0, alpha)
        m_s[...] = m_new

    def block_of(u):
        return jnp.where(jnp.logical_and(odd == 0, u == 0), i, u - 1 + odd)

    last = i - odd
    first_mask = jnp.logical_or(causal, odd == 1)
    bmax = score(block_of(0), s0, first_mask)
    m, alpha = soft(s0, p0, m_s[...], bmax)
    bmax = score(block_of(jnp.minimum(1, last)), s1, None)

    def body(d, carry):
        m, alpha, bmax = carry
        u = 2 * d
        accum(block_of(u), p0, alpha)
        m, alpha = soft(s1, p1, m, bmax)
        bmax = score(block_of(u + 2), s0, None)
        accum(block_of(u + 1), p1, alpha)
        m, alpha = soft(s0, p0, m, bmax)
        bmax = score(block_of(jnp.minimum(u + 3, last)), s1, None)
        return m, alpha, bmax

    m, alpha, _ = lax.fori_loop(0, last // 2, body, (m, alpha, bmax))
    accum(block_of(last), p0, alpha)
    acc = acc_s[...]
    out_t = acc[0:F_DH, :] / acc[FOX_VT_ONE:FOX_VT_ONE + 1, :]
    o_ref[...] = (out_t.T * _sigmoid(og_ref[...].astype(F32))).astype(BF16)


def _fox(z, fcum):
    s = z.shape[0]
    T = FOX_T
    cpb = PK_BLOCK // F_DH
    return pl.pallas_call(
        _fox_kernel,
        out_shape=jax.ShapeDtypeStruct((s, F_W), BF16),
        grid=(F_HEADS, s // T),
        in_specs=[
            pl.BlockSpec((T, F_DH), lambda h, i: (i, BLK_FQ * cpb + h)),
            pl.BlockSpec((s, F_DH), lambda h, i: (0, BLK_FK * cpb + h)),
            pl.BlockSpec((s, F_DH), lambda h, i: (0, BLK_FV * cpb + h)),
            pl.BlockSpec((s, LANES), lambda h, i: (0, 0)),
            pl.BlockSpec((T, F_DH), lambda h, i: (i, BLK_FO * cpb + h)),
        ],
        out_specs=pl.BlockSpec((T, F_DH), lambda h, i: (i, h)),
        scratch_shapes=[pltpu.VMEM((s, FOX_KX), BF16), pltpu.VMEM((FOX_VT, s), BF16),
                        pltpu.VMEM((T, T), F32), pltpu.VMEM((T, T), F32),
                        pltpu.VMEM((T, T), BF16), pltpu.VMEM((T, T), BF16),
                        pltpu.VMEM((FOX_VT, T), F32), pltpu.VMEM((1, T), F32)],
        compiler_params=_cparams(("arbitrary", "arbitrary")),
        name="fox_attention",
    )(z, z, z, fcum, z)


def _outproj_kernel(hm_ref, hf_ref, ga0_ref, ga1_ref, gb0_ref, gb1_ref, x_ref, wm_ref, wf_ref, wo_ref,
                    g1_ref, n2_ref, sc2_ref, sh2_ref, wr_ref, br_ref,
                    x1_ref, h2_ref, lg_ref):
    ym = jnp.dot(hm_ref[...], wm_ref[...], preferred_element_type=F32)
    yf = jnp.dot(hf_ref[...], wf_ref[...], preferred_element_type=F32)
    ga = jnp.concatenate([ga0_ref[...], ga1_ref[...]], axis=1).astype(F32)
    gb = jnp.concatenate([gb0_ref[...], gb1_ref[...]], axis=1).astype(F32)
    y = _sigmoid(ga) * ym + _sigmoid(gb) * yf
    mix = jnp.dot(y.astype(BF16), wo_ref[...], preferred_element_type=F32)
    x1 = x_ref[...] + g1_ref[...] * mix
    x1_ref[...] = x1
    hn = x1 * lax.rsqrt(jnp.mean(x1 * x1, axis=-1, keepdims=True) + EPS) * n2_ref[...]
    h2 = hn * (1.0 + sc2_ref[...]) + sh2_ref[...]
    h2_ref[...] = h2
    lg_ref[...] = _dot_split(h2, wr_ref) + br_ref[...]


def _outproj(hm, hf, z, x, wm, wf, wo, g1, n2, sc2, sh2, wr, br):
    s, d = x.shape
    tm = OUT_TM
    const = lambda m: (0, 0)
    return pl.pallas_call(
        _outproj_kernel,
        out_shape=(jax.ShapeDtypeStruct((s, d), F32),
                   jax.ShapeDtypeStruct((s, d), F32),
                   jax.ShapeDtypeStruct((s, LANES), F32)),
        grid=(s // tm,),
        in_specs=[
            pl.BlockSpec((tm, M_V_W), lambda m: (m, 0)),
            pl.BlockSpec((tm, F_W), lambda m: (m, 0)),
            pl.BlockSpec((tm, PK_BLOCK), lambda m: (m, BLK_GA)),
            pl.BlockSpec((tm, PK_BLOCK), lambda m: (m, BLK_GA + 1)),
            pl.BlockSpec((tm, PK_BLOCK), lambda m: (m, BLK_GB)),
            pl.BlockSpec((tm, PK_BLOCK), lambda m: (m, BLK_GB + 1)),
            pl.BlockSpec((tm, d), lambda m: (m, 0)),
            pl.BlockSpec((M_V_W, d), const, pipeline_mode=pl.Buffered(1)),
            pl.BlockSpec((F_W, d), const, pipeline_mode=pl.Buffered(1)),
            pl.BlockSpec((d, d), const, pipeline_mode=pl.Buffered(1)),
            pl.BlockSpec((1, d), const),
            pl.BlockSpec((1, d), const),
            pl.BlockSpec((1, d), const),
            pl.BlockSpec((1, d), const),
            pl.BlockSpec((d, 2 * LANES), const),
            pl.BlockSpec((1, LANES), const),
        ],
        out_specs=(pl.BlockSpec((tm, d), lambda m: (m, 0)),
                   pl.BlockSpec((tm, d), lambda m: (m, 0)),
                   pl.BlockSpec((tm, LANES), lambda m: (m, 0))),
        compiler_params=_cparams(("arbitrary",)),
        name="outproj",
    )(hm, hf, z, z, z, z, x, wm, wf, wo, g1, n2, sc2, sh2, wr, br)


def _group_tables(tile_e, n_used):
    nt = tile_e.shape[0]
    idx = jnp.arange(nt, dtype=jnp.int32)
    used = idx < n_used[0]
    prev = jnp.concatenate([tile_e[:1] - 1, tile_e[:-1]])
    first = jnp.logical_and(used, tile_e != prev).astype(jnp.int32)
    gseq = jnp.cumsum(first) - 1
    own = jnp.logical_and(first[None, :] == 1, gseq[None, :] == idx[:, None])
    gexp = jnp.sum(jnp.where(own, tile_e[None, :], 0), axis=1)
    return gseq.astype(jnp.int32), gexp.astype(jnp.int32), jnp.sum(first).reshape(1)


def _stream_group_weights(i, chunk, n_chunks, gseq_ref, ng_ref, copies_of):
    ng = ng_ref[0]
    g = chunk * ng + gseq_ref[i]
    slot = g % 2
    first = jnp.logical_or(i == 0, gseq_ref[i] != gseq_ref[jnp.maximum(i - 1, 0)])

    @pl.when(first)
    def _():
        @pl.when(g == 0)
        def _():
            for cp in copies_of(g, slot):
                cp.start()

        @pl.when(g + 1 < n_chunks * ng)
        def _():
            for cp in copies_of(g + 1, 1 - slot):
                cp.start()

        for cp in copies_of(g, slot):
            cp.wait()

    return slot


def _moe_gu_kernel(te_ref, nu_ref, gseq_ref, gexp_ref, ng_ref, x_ref, w_hbm, bg_ref, bu_ref,
                   o_ref, wbuf, sem):
    f = pl.program_id(0)
    i = pl.program_id(1)

    def copies_of(g, slot):
        fg = g // ng_ref[0]
        e = gexp_ref[g - fg * ng_ref[0]]
        col = pl.multiple_of(fg * MOE_TF, MOE_TF)
        return (pltpu.make_async_copy(w_hbm.at[e, :, pl.ds(col, MOE_TF)],
                                      wbuf.at[slot, 0], sem.at[slot]),
                pltpu.make_async_copy(w_hbm.at[e, :, pl.ds(D_FF + col, MOE_TF)],
                                      wbuf.at[slot, 1], sem.at[slot]))

    @pl.when(i < nu_ref[0])
    def _():
        slot = _stream_group_weights(i, f, pl.num_programs(0), gseq_ref, ng_ref, copies_of)
        x = x_ref[...].astype(BF16)
        g = jnp.dot(x, wbuf[slot, 0].astype(BF16), preferred_element_type=F32) + bg_ref[0]
        u = jnp.dot(x, wbuf[slot, 1].astype(BF16), preferred_element_type=F32) + bu_ref[0]
        gate = jnp.minimum(g, SWIGLU_LIMIT)
        up = jnp.clip(u, -SWIGLU_LIMIT, SWIGLU_LIMIT)
        act = (up + 1.0) * gate * _sigmoid(SWIGLU_ALPHA * gate)
        o_ref[...] = act.astype(BF16)

    @pl.when(i >= nu_ref[0])
    def _():
        o_ref[...] = jnp.zeros(o_ref.shape, BF16)


def _moe_gate_up(tables, xs, w_gu, b_gu):
    r, d = xs.shape
    nf = D_FF // MOE_TF
    grid_spec = pltpu.PrefetchScalarGridSpec(
        num_scalar_prefetch=len(tables),
        grid=(nf, r // MOE_TM),
        in_specs=[
            pl.BlockSpec((MOE_TM, d), lambda f, i, te, nu, *_: (jnp.minimum(i, nu[0] - 1), 0)),
            pl.BlockSpec(memory_space=pl.ANY),
            pl.BlockSpec((1, 1, MOE_TF), lambda f, i, te, *_: (te[i], 0, f)),
            pl.BlockSpec((1, 1, MOE_TF), lambda f, i, te, *_: (te[i], 0, nf + f)),
        ],
        out_specs=pl.BlockSpec((MOE_TM, MOE_TF), lambda f, i, *_: (i, f)),
        scratch_shapes=[pltpu.VMEM((2, 2, d, MOE_TF), F32), pltpu.SemaphoreType.DMA((2,))],
    )
    return pl.pallas_call(
        _moe_gu_kernel,
        out_shape=jax.ShapeDtypeStruct((r, D_FF), BF16),
        grid_spec=grid_spec,
        compiler_params=_cparams(("arbitrary", "arbitrary")),
        name="moe_gate_up",
    )(*tables, xs, w_gu, b_gu, b_gu)


def _moe_down_kernel(te_ref, nu_ref, gseq_ref, gexp_ref, ng_ref, a_ref, w_hbm, bd_ref, o_ref,
                     wbuf, sem):
    n = pl.program_id(0)
    i = pl.program_id(1)

    def copies_of(g, slot):
        ch = g // ng_ref[0]
        e = gexp_ref[g - ch * ng_ref[0]]
        col = pl.multiple_of(ch * MOE_TN, MOE_TN)
        return (pltpu.make_async_copy(w_hbm.at[e, :, pl.ds(col, MOE_TN)],
                                      wbuf.at[slot], sem.at[slot]),)

    @pl.when(i < nu_ref[0])
    def _():
        slot = _stream_group_weights(i, n, pl.num_programs(0), gseq_ref, ng_ref, copies_of)
        o_ref[...] = (jnp.dot(a_ref[...], wbuf[slot].astype(BF16), preferred_element_type=F32)
                      + bd_ref[0])

    @pl.when(i >= nu_ref[0])
    def _():
        o_ref[...] = jnp.zeros(o_ref.shape, F32)


def _moe_down(tables, act, w_d, b_d):
    r = act.shape[0]
    nn = D_MODEL // MOE_TN
    grid_spec = pltpu.PrefetchScalarGridSpec(
        num_scalar_prefetch=len(tables),
        grid=(nn, r // MOE_TM),
        in_specs=[
            pl.BlockSpec((MOE_TM, D_FF), lambda n, i, te, nu, *_: (jnp.minimum(i, nu[0] - 1), 0)),
            pl.BlockSpec(memory_space=pl.ANY),
            pl.BlockSpec((1, 1, MOE_TN), lambda n, i, te, *_: (te[i], 0, n)),
        ],
        out_specs=pl.BlockSpec((MOE_TM, MOE_TN), lambda n, i, *_: (i, n)),
        scratch_shapes=[pltpu.VMEM((2, D_FF, MOE_TN), F32), pltpu.SemaphoreType.DMA((2,))],
    )
    return pl.pallas_call(
        _moe_down_kernel,
        out_shape=jax.ShapeDtypeStruct((r, D_MODEL), F32),
        grid_spec=grid_spec,
        compiler_params=_cparams(("arbitrary", "arbitrary")),
        name="moe_down",
    )(*tables, act, w_d, b_d)


def _final_kernel(pos_ref, x1_ref, w_ref, g2_ref, gf_ref, y_hbm, o_ref, buf, sem):
    m = pl.program_id(0)
    nm = pl.num_programs(0)
    tm = x1_ref.shape[0]

    def issue(step, slot):
        base = step * (tm * TOP_K)

        def body(t, carry):
            for k in range(TOP_K):
                r = pos_ref[base + t * TOP_K + k]
                pltpu.make_async_copy(y_hbm.at[pl.ds(r, 1), :],
                                      buf.at[slot, k, pl.ds(t, 1), :], sem.at[slot]).start()
            return carry

        lax.fori_loop(0, tm, body, 0, unroll=DMA_ISSUE_UNROLL)

    @pl.when(m == 0)
    def _():
        issue(0, 0)

    @pl.when(m + 1 < nm)
    def _():
        issue(m + 1, (m + 1) % 2)

    slot = m % 2
    for k in range(TOP_K):
        pltpu.make_async_copy(y_hbm.at[pl.ds(0, tm), :], buf.at[slot, k], sem.at[slot]).wait()
    w = w_ref[...]
    moe = w[:, 0:1] * buf[slot, 0]
    for k in range(1, TOP_K):
        moe = moe + w[:, k:k + 1] * buf[slot, k]
    x2 = x1_ref[...] + g2_ref[...] * moe
    o_ref[...] = x2 * lax.rsqrt(jnp.mean(x2 * x2, axis=-1, keepdims=True) + EPS) * gf_ref[...]


def _final(pos_flat, x1, w_top, g2, gf, ys):
    s, d = x1.shape
    tm = FIN_TM
    const = lambda m, pos: (0, 0)
    grid_spec = pltpu.PrefetchScalarGridSpec(
        num_scalar_prefetch=1,
        grid=(s // tm,),
        in_specs=[
            pl.BlockSpec((tm, d), lambda m, pos: (m, 0)),
            pl.BlockSpec((tm, LANES), lambda m, pos: (m, 0)),
            pl.BlockSpec((1, d), const),
            pl.BlockSpec((1, d), const),
            pl.BlockSpec(memory_space=pl.ANY),
        ],
        out_specs=pl.BlockSpec((tm, d), lambda m, pos: (m, 0)),
        scratch_shapes=[pltpu.VMEM((2, TOP_K, tm, d), F32), pltpu.SemaphoreType.DMA((2,))],
    )
    return pl.pallas_call(
        _final_kernel,
        out_shape=jax.ShapeDtypeStruct((s, d), F32),
        grid_spec=grid_spec,
        compiler_params=_cparams(("arbitrary",)),
        name="final_combine",
    )(pos_flat, x1, w_top, g2, gf, ys)


ROUTE_TM = 512
META_NUSED, META_PAD0, META_PADN = 0, 1, 2


def _route_kernel(lg_ref, pos_ref, w_ref, te_ref, nu_ref, cnt_s, pstart_s, carry_s):
    p = pl.program_id(0)
    m = pl.program_id(1)
    tm = lg_ref.shape[0]
    lane = lax.broadcasted_iota(jnp.int32, (tm, LANES), 1)
    lanef = lane.astype(F32)
    l = jnp.where(lane < N_EXPERTS, lg_ref[...], -jnp.inf)
    sel = jnp.zeros((tm, LANES), F32)
    vals, hots = [], []
    for _ in range(TOP_K):
        mx = jnp.max(l, axis=1, keepdims=True)
        idx = jnp.min(jnp.where(l == mx, lanef, float(LANES)), axis=1, keepdims=True)
        hot = lanef == idx
        vals.append(mx)
        hots.append(hot)
        l = jnp.where(hot, -jnp.inf, l)
        sel = sel + jnp.where(hot, 1.0, 0.0)
    colsum = jnp.sum(sel, axis=0, keepdims=True)

    @pl.when(p == 0)
    def _():
        @pl.when(m == 0)
        def _():
            cnt_s[...] = jnp.zeros(cnt_s.shape, F32)

        cnt_s[...] = cnt_s[...] + colsum

    @pl.when(jnp.logical_and(p == 1, m == 0))
    def _():
        cnt = cnt_s[...]
        padded = jnp.floor((cnt + (MOE_TM - 1.0)) * (1.0 / MOE_TM)) * MOE_TM
        lane8 = lax.broadcasted_iota(jnp.int32, cnt.shape, 1)
        pend = padded
        d = 1
        while d < LANES:
            pend = pend + jnp.where(lane8 >= d, pltpu.roll(pend, d, axis=1), 0.0)
            d *= 2
        pstart_s[...] = pend - padded
        carry_s[...] = jnp.zeros(carry_s.shape, F32)
        total = pend[:, N_EXPERTS - 1:N_EXPERTS]
        nt = te_ref.shape[0]
        tstart = lax.broadcasted_iota(jnp.int32, (nt, LANES), 0).astype(F32) * MOE_TM
        lane_t = lax.broadcasted_iota(jnp.int32, (nt, LANES), 1)
        done = jnp.where(lane_t < N_EXPERTS, jnp.where(tstart >= pend[0:1, :], 1.0, 0.0), 0.0)
        te = jnp.sum(done, axis=1, keepdims=True)
        last = jnp.sum(jnp.where(lane8 < N_EXPERTS,
                                 jnp.where(pend <= total - 1.0, 1.0, 0.0), 0.0),
                       axis=1, keepdims=True)
        te_ref[...] = jnp.minimum(te, last[0:1, :]).astype(jnp.int32)
        mrow = lax.broadcasted_iota(jnp.int32, cnt.shape, 0)
        meta = jnp.where(mrow == META_NUSED, jnp.broadcast_to(total * (1.0 / MOE_TM), cnt.shape),
                         jnp.where(mrow == META_PAD0, pend - padded + cnt, padded - cnt))
        nu_ref[...] = meta.astype(jnp.int32)

    @pl.when(p == 1)
    def _():
        r = lax.broadcasted_iota(jnp.int32, (tm, tm), 0)
        c = lax.broadcasted_iota(jnp.int32, (tm, tm), 1)
        tri = jnp.where(r > c, 1.0, 0.0).astype(BF16)
        rowbase = (jnp.dot(tri, sel.astype(BF16), preferred_element_type=F32)
                   + carry_s[0:1, :] + pstart_s[0:1, :])
        carry_s[...] = carry_s[...] + colsum
        den = jnp.exp(vals[0] - vals[0])
        for k in range(1, TOP_K):
            den = den + jnp.exp(vals[k] - vals[0])
        posf = jnp.zeros((tm, LANES), F32)
        wf = jnp.zeros((tm, LANES), F32)
        for k in range(TOP_K):
            pk = jnp.sum(jnp.where(hots[k], rowbase, 0.0), axis=1, keepdims=True)
            wk = jnp.exp(vals[k] - vals[0]) / den
            posf = jnp.where(lane == k, pk, posf)
            wf = jnp.where(lane == k, wk, wf)
        pos_ref[...] = posf.astype(jnp.int32)
        w_ref[...] = wf


def _route(lg, n_tiles):
    s = lg.shape[0]
    tm = ROUTE_TM
    nt_pad = -(-n_tiles // 8) * 8
    const = lambda p, m: (0, 0)
    return pl.pallas_call(
        _route_kernel,
        out_shape=(jax.ShapeDtypeStruct((s, LANES), jnp.int32),
                   jax.ShapeDtypeStruct((s, LANES), F32),
                   jax.ShapeDtypeStruct((nt_pad, 1), jnp.int32),
                   jax.ShapeDtypeStruct((8, LANES), jnp.int32)),
        grid=(2, s // tm),
        in_specs=[pl.BlockSpec((tm, LANES), lambda p, m: (m, 0))],
        out_specs=(pl.BlockSpec((tm, LANES), lambda p, m: (m * p, 0)),
                   pl.BlockSpec((tm, LANES), lambda p, m: (m * p, 0)),
                   pl.BlockSpec((nt_pad, 1), const),
                   pl.BlockSpec((8, LANES), const)),
        scratch_shapes=[pltpu.VMEM((8, LANES), F32), pltpu.VMEM((8, LANES), F32),
                        pltpu.VMEM((8, LANES), F32)],
        compiler_params=_cparams(("arbitrary", "arbitrary")),
        name="route",
    )(lg)


DISP_TM = 256


ZERO_ROWS = MOE_TM // 2


def _dispatch_kernel(pos_ref, pad0_ref, padn_ref, nu_ref, h_ref, xs_hbm, zero_s, sem, zsem):
    m = pl.program_id(0)
    tm = h_ref.shape[0]
    base = m * (tm * TOP_K)
    n_tiles = xs_hbm.shape[0] // MOE_TM

    def pad_copies(e, act):
        start = pad0_ref[e]
        n = padn_ref[e]
        head = jnp.minimum(n, (-start) & (SUBLANES - 1))

        def one_row(r, c):
            act(pltpu.make_async_copy(zero_s.at[pl.ds(0, 1), :],
                                      xs_hbm.at[pl.ds(start + r, 1), :], zsem))
            return c

        lax.fori_loop(0, head, one_row, 0)
        start = start + head
        n = n - head
        b = ZERO_ROWS
        while b >= SUBLANES:
            @pl.when((n & b) != 0)
            def _(b=b):
                off = pl.multiple_of(start + (n & ~(2 * b - 1)), SUBLANES)
                act(pltpu.make_async_copy(zero_s.at[pl.ds(0, b), :],
                                          xs_hbm.at[pl.ds(off, b), :], zsem))
            b //= 2

    def tail_copies(j, act):
        for half in range(MOE_TM // ZERO_ROWS):
            off = pl.multiple_of(j * MOE_TM + half * ZERO_ROWS, ZERO_ROWS)
            act(pltpu.make_async_copy(zero_s, xs_hbm.at[pl.ds(off, ZERO_ROWS), :], zsem))

    @pl.when(m == 0)
    def _():
        zero_s[...] = jnp.zeros(zero_s.shape, F32)
        for act in (lambda cp: cp.start(), lambda cp: cp.wait()):
            lax.fori_loop(0, N_EXPERTS, lambda e, c, act=act: (pad_copies(e, act), c)[1], 0)
            lax.fori_loop(nu_ref[0], n_tiles, lambda j, c, act=act: (tail_copies(j, act), c)[1], 0)

    def body(t, carry):
        for k in range(TOP_K):
            r = pos_ref[base + t * TOP_K + k]
            pltpu.make_async_copy(h_ref.at[pl.ds(t, 1), :], xs_hbm.at[pl.ds(r, 1), :], sem).start()
        return carry

    lax.fori_loop(0, tm, body, 0, unroll=DMA_ISSUE_UNROLL)
    for _ in range(TOP_K):
        pltpu.make_async_copy(h_ref, xs_hbm.at[pl.ds(0, tm), :], sem).wait()


def _dispatch(pos_flat, pad0, padn, n_used, h2, n_rows):
    s, d = h2.shape
    tm = DISP_TM
    grid_spec = pltpu.PrefetchScalarGridSpec(
        num_scalar_prefetch=4,
        grid=(s // tm,),
        in_specs=[pl.BlockSpec((tm, d), lambda m, *_: (m, 0))],
        out_specs=pl.BlockSpec(memory_space=pl.ANY),
        scratch_shapes=[pltpu.VMEM((ZERO_ROWS, d), F32), pltpu.SemaphoreType.DMA(()),
                        pltpu.SemaphoreType.DMA(())],
    )
    return pl.pallas_call(
        _dispatch_kernel,
        out_shape=jax.ShapeDtypeStruct((n_rows, d), F32),
        grid_spec=grid_spec,
        compiler_params=_cparams(("arbitrary",)),
        name="dispatch",
    )(pos_flat, pad0, padn, n_used, h2)


def _gate_weights(w_in):
    return jnp.concatenate([w_in[:, W_IN_IGATE:W_IN_IGATE + 2 * M_HEADS],
                            w_in[:, W_IN_FGATE:W_IN_FGATE + F_HEADS],
                            jnp.zeros((w_in.shape[0], LANES - G_ROWS), F32)], axis=1)


def kernel(x, c, ada_w, ada_b, norm1_g, w_in, m_conv_w, m_conv_b, m_i_bias, m_f_bias,
           m_out_norm_g, f_f_bias, f_q_norm_g, f_k_norm_g, w_branch_m, w_branch_f, w_out,
           norm2_g, w_router, b_router, w_gate_up, b_gate_up, w_down, b_down, final_norm_g):
    b, s, d = x.shape
    assert b == 1 and ada_w.shape[0] == 1
    x2d = x.reshape(s, d)

    mod = _ada_mod(c.reshape(d, 1), ada_w.reshape(d, 6 * d), ada_b.reshape(1, 6 * d))
    sh1, sc1, g1, sh2, sc2, g2 = [mod[:, i * d:(i + 1) * d] for i in range(6)]

    w_in2d = w_in.reshape(d, -1)
    wg2 = jnp.concatenate(_split_bf16(_gate_weights(w_in2d)), axis=1)
    h1, zg = _norm1(x2d, norm1_g.reshape(1, d), sc1, sh1, wg2)
    z = _inproj(h1, w_in2d, f_q_norm_g.reshape(1, F_W), f_k_norm_g.reshape(1, F_W))

    gate_bias = jnp.concatenate([m_i_bias.reshape(1, M_HEADS), m_f_bias.reshape(1, M_HEADS),
                                 f_f_bias.reshape(1, F_HEADS),
                                 jnp.zeros((1, LANES - G_ROWS), F32)], axis=1)
    gcol, grow = _gates(zg, gate_bias)

    hm = _mlstm(z, gcol, grow, m_conv_w.reshape(M_CONV, 2 * M_QK_W),
                m_conv_b.reshape(1, 2 * M_QK_W), m_out_norm_g.reshape(1, M_V_W))
    hf = _fox(z, gcol)

    wr = jnp.concatenate([w_router.reshape(d, N_EXPERTS),
                          jnp.zeros((d, LANES - N_EXPERTS), F32)], axis=1)
    br = jnp.concatenate([b_router.reshape(1, N_EXPERTS),
                          jnp.zeros((1, LANES - N_EXPERTS), F32)], axis=1)
    x1, h2, lg = _outproj(hm, hf, z, x2d,
                          w_branch_m.reshape(M_V_W, d).astype(BF16),
                          w_branch_f.reshape(F_W, d).astype(BF16),
                          w_out.reshape(d, d).astype(BF16),
                          g1, norm2_g.reshape(1, d), sc2, sh2,
                          jnp.concatenate(_split_bf16(wr), axis=1), br)

    n_rows = s * TOP_K + N_EXPERTS * MOE_TM
    n_tiles = n_rows // MOE_TM
    pos, w_top, te, meta = _route(lg, n_tiles)
    pos_flat = pos[:, :TOP_K].reshape(s * TOP_K)
    tile_e = te[:n_tiles, 0]
    n_used = meta[META_NUSED, :1]
    xs = _dispatch(pos_flat, meta[META_PAD0, :N_EXPERTS], meta[META_PADN, :N_EXPERTS], n_used,
                   h2, n_rows)
    tables = (tile_e, n_used) + _group_tables(tile_e, n_used)
    act = _moe_gate_up(tables, xs, w_gate_up.reshape(N_EXPERTS, d, 2 * D_FF),
                       b_gate_up.reshape(N_EXPERTS, 1, 2 * D_FF))
    ys = _moe_down(tables, act, w_down.reshape(N_EXPERTS, D_FF, d),
                   b_down.reshape(N_EXPERTS, 1, d))
    out = _final(pos_flat, x1, w_top, g2, final_norm_g.reshape(1, d), ys)
    return out.reshape(b, s, d)
```

```python
import functools

import jax
import jax.numpy as jnp
from jax import lax
from jax.experimental import pallas as pl
from jax.experimental.pallas import tpu as pltpu

F32 = jnp.float32
BF16 = jnp.bfloat16

D_MODEL = 2048
M_HEADS = 8
M_DQK = 64
M_DV = 128
M_CONV = 4
F_HEADS = 8
F_DH = 128
N_EXPERTS = 32
TOP_K = 4
D_FF = 2048
SWIGLU_LIMIT = 7.0
SWIGLU_ALPHA = 1.702
EPS = 1e-6
LOG2E = 1.4426950408889634

M_QK_W = M_HEADS * M_DQK
M_V_W = M_HEADS * M_DV
F_W = F_HEADS * F_DH

VMEM_LIMIT = 56 * 1024 * 1024
LANES = 128
SUBLANES = 8

PK_BLOCK = 1024
PK_NBLK = 11
BLK_MQK, BLK_MV, BLK_MO, BLK_FQ, BLK_FK, BLK_FV, BLK_FO, BLK_GA, BLK_GB = 0, 1, 2, 3, 4, 5, 6, 7, 9
W_IN_IGATE = 2 * M_QK_W + M_V_W
W_IN_FGATE = W_IN_IGATE + 2 * M_HEADS + M_V_W + 3 * F_W
PK_SHIFT_GROUPS = ((0, BLK_MQK, BLK_MO), (2 * M_HEADS, BLK_MO, BLK_FO),
                   (2 * M_HEADS + F_HEADS, BLK_FO, PK_NBLK))

NORM_TM = 512
IN_TM = 1024
MLSTM_L = 256
FOX_TK = 512
FOX_TQ = 2 * FOX_TK
OUT_TM = 256
MOE_TM = 256
MOE_TF = 1024
MOE_TN = 1024
FIN_TM = 256
DMA_ISSUE_UNROLL = 4


def _cparams(sem):
    return pltpu.CompilerParams(dimension_semantics=sem, vmem_limit_bytes=VMEM_LIMIT)


def _sigmoid(x):
    return 1.0 / (1.0 + jnp.exp(-x))


def _split_bf16(a):
    c = a * 65537.0
    hi = c - (c - a)
    return hi.astype(BF16), (a - hi).astype(BF16)


def _dot_split(a, w2_ref, transposed=False):
    a_hi, a_lo = _split_bf16(a)
    if transposed:
        nt = (((1,), (1,)), ((), ()))
        zz = lax.dot_general(a_hi, w2_ref[...], nt, preferred_element_type=F32)
        lo = lax.dot_general(a_lo, w2_ref[:LANES, :], nt, preferred_element_type=F32)
    else:
        zz = jnp.dot(a_hi, w2_ref[...], preferred_element_type=F32)
        lo = jnp.dot(a_lo, w2_ref[:, :LANES], preferred_element_type=F32)
    return zz[:, :LANES] + zz[:, LANES:] + lo


ADA_TN = 1024


def _ada_kernel(c_ref, w_ref, b_ref, o_ref):
    nrow = w_ref.shape[0] // 8

    def body(r, acc):
        rows = pl.ds(pl.multiple_of(r * 8, 8), 8)
        c = c_ref[rows, :]
        sc = c * _sigmoid(c)
        return acc + sc * w_ref[rows, :]

    acc = lax.fori_loop(0, nrow, body, jnp.zeros((8, w_ref.shape[1]), F32), unroll=8)
    o_ref[...] = jnp.sum(acc, axis=0, keepdims=True) + b_ref[...]


def _ada_mod(c_col, ada_w, ada_b):
    d, n = ada_w.shape
    return pl.pallas_call(
        _ada_kernel,
        out_shape=jax.ShapeDtypeStruct((1, n), F32),
        grid=(n // ADA_TN,),
        in_specs=[
            pl.BlockSpec((d, 1), lambda j: (0, 0)),
            pl.BlockSpec((d, ADA_TN), lambda j: (0, j)),
            pl.BlockSpec((1, ADA_TN), lambda j: (0, j)),
        ],
        out_specs=pl.BlockSpec((1, ADA_TN), lambda j: (0, j)),
        compiler_params=_cparams(("arbitrary",)),
        name="ada_mod",
    )(c_col, ada_w, ada_b)


def _head_rmsnorm(a, g_row, scale):
    outs = []
    for h in range(F_HEADS):
        ah = a[:, h * F_DH:(h + 1) * F_DH]
        ms = jnp.mean(ah * ah, axis=-1, keepdims=True)
        outs.append(ah * lax.rsqrt(ms + EPS) * (g_row[:, h * F_DH:(h + 1) * F_DH] * scale))
    return jnp.concatenate(outs, axis=-1)


def _norm1_kernel(x_ref, n1_ref, sc1_ref, sh1_ref, wg_ref, h_ref, zg_ref):
    x = x_ref[...]
    y = x * lax.rsqrt(jnp.mean(x * x, axis=-1, keepdims=True) + EPS) * n1_ref[...]
    h = y * (1.0 + sc1_ref[...]) + sh1_ref[...]
    h_ref[...] = h.astype(BF16)
    zg_ref[...] = _dot_split(h, wg_ref, transposed=True)


def _norm1(x, n1, sc1, sh1, w_gates):
    s, d = x.shape
    tm = NORM_TM
    row = lambda m: (0, 0)
    return pl.pallas_call(
        _norm1_kernel,
        out_shape=(jax.ShapeDtypeStruct((s, d), BF16), jax.ShapeDtypeStruct((s, LANES), F32)),
        grid=(s // tm,),
        in_specs=[pl.BlockSpec((tm, d), lambda m: (m, 0)), pl.BlockSpec((1, d), row),
                  pl.BlockSpec((1, d), row), pl.BlockSpec((1, d), row),
                  pl.BlockSpec((2 * LANES, d), row)],
        out_specs=(pl.BlockSpec((tm, d), lambda m: (m, 0)),
                   pl.BlockSpec((tm, LANES), lambda m: (m, 0))),
        compiler_params=_cparams(("arbitrary",)),
        name="norm1",
    )(x, n1, sc1, sh1, w_gates)


def _inproj_kernel(h_ref, wa_ref, wb_ref, qg_ref, kg_ref, z_ref, w_s):
    n = pl.program_id(0)
    m = pl.program_id(1)

    for shift, lo, hi in PK_SHIFT_GROUPS:
        @pl.when(jnp.logical_and(m == 0, jnp.logical_and(n >= lo, n < hi)))
        def _(shift=shift):
            if shift == 0:
                w_s[...] = wa_ref[...].astype(BF16)
            else:
                w_s[...] = jnp.concatenate([wa_ref[shift:, :], wb_ref[:shift, :]],
                                           axis=0).astype(BF16)

    acc = lax.dot_general(h_ref[...], w_s[...], (((1,), (1,)), ((), ())),
                          preferred_element_type=F32)

    @pl.when(n == BLK_FQ)
    def _():
        z_ref[...] = _head_rmsnorm(acc, qg_ref[...], F_DH ** -0.5 * LOG2E).astype(BF16)

    @pl.when(n == BLK_FK)
    def _():
        z_ref[...] = _head_rmsnorm(acc, kg_ref[...], 1.0).astype(BF16)

    @pl.when(jnp.logical_and(n != BLK_FQ, n != BLK_FK))
    def _():
        z_ref[...] = acc.astype(BF16)


def _inproj(h, w_in, qg, kg):
    s, d = h.shape
    row = lambda n, m: (0, 0)
    lpb = PK_BLOCK // LANES
    return pl.pallas_call(
        _inproj_kernel,
        out_shape=jax.ShapeDtypeStruct((s, PK_NBLK * PK_BLOCK), BF16),
        grid=(PK_NBLK, s // IN_TM),
        in_specs=[
            pl.BlockSpec((IN_TM, d), lambda n, m: (m, 0)),
            pl.BlockSpec((PK_BLOCK, d), lambda n, m: (n, 0)),
            pl.BlockSpec((LANES, d), lambda n, m: (lpb * (n + 1), 0)),
            pl.BlockSpec((1, PK_BLOCK), row),
            pl.BlockSpec((1, PK_BLOCK), row),
        ],
        out_specs=pl.BlockSpec((IN_TM, PK_BLOCK), lambda n, m: (m, n)),
        scratch_shapes=[pltpu.VMEM((PK_BLOCK, d), BF16)],
        compiler_params=_cparams(("arbitrary", "arbitrary")),
        name="inproj",
    )(h, w_in, w_in, qg, kg)


G_IPRE, G_BM, G_FF = 0, M_HEADS, 2 * M_HEADS
G_ROWS = 3 * M_HEADS
GATE_TM = 512


def _gates_kernel(zg_ref, b_ref, gcol_ref, grow_ref, carry_s):
    t = pl.program_id(0)
    tm = zg_ref.shape[0]

    @pl.when(t == 0)
    def _():
        carry_s[...] = jnp.zeros(carry_s.shape, F32)

    a = zg_ref[...] + b_ref[...]
    lane = lax.broadcasted_iota(jnp.int32, a.shape, 1)
    logf = jnp.minimum(a, 0.0) - jnp.log1p(jnp.exp(-jnp.abs(a)))
    is_f = jnp.logical_and(lane >= G_BM, lane < G_ROWS)
    r = lax.broadcasted_iota(jnp.int32, (tm, tm), 0)
    c = lax.broadcasted_iota(jnp.int32, (tm, tm), 1)
    tri = jnp.where(r >= c, 1.0, 0.0)
    cum = jnp.dot(tri, jnp.where(is_f, logf, 0.0), preferred_element_type=F32,
                  precision=lax.Precision.HIGHEST) + carry_s[0:1, :]
    carry_s[...] = jnp.broadcast_to(cum[tm - 1:tm, :], carry_s.shape)
    g = jnp.where(lane < G_BM, a, cum)
    gcol_ref[...] = g
    grow_ref[...] = g.T[0:G_ROWS, :]


def _gates(zg, bias_row):
    s = zg.shape[0]
    tm = GATE_TM
    return pl.pallas_call(
        _gates_kernel,
        out_shape=(jax.ShapeDtypeStruct((s, LANES), F32),
                   jax.ShapeDtypeStruct((G_ROWS, s), F32)),
        grid=(s // tm,),
        in_specs=[pl.BlockSpec((tm, LANES), lambda t: (t, 0)),
                  pl.BlockSpec((1, LANES), lambda t: (0, 0))],
        out_specs=(pl.BlockSpec((tm, LANES), lambda t: (t, 0)),
                   pl.BlockSpec((G_ROWS, tm), lambda t: (0, t))),
        scratch_shapes=[pltpu.VMEM((8, LANES), F32)],
        compiler_params=_cparams(("arbitrary",)),
        name="gates",
    )(zg, bias_row)


def _mlstm_kernel(qk_ref, v_ref, og_ref, gcol_ref, irow_ref, brow_ref,
                  cw_ref, cb_ref, gn_ref, out_ref, xbuf, c_s, m_s, bend_s):
    c = pl.program_id(0)
    L = qk_ref.shape[0]

    @pl.when(c == 0)
    def _():
        xbuf[0:8, :] = jnp.zeros((8, xbuf.shape[1]), F32)
        c_s[...] = jnp.zeros(c_s.shape, F32)
        m_s[...] = jnp.zeros(m_s.shape, F32)
        bend_s[...] = jnp.zeros(bend_s.shape, F32)

    xbuf[8:8 + L, :] = qk_ref[...].astype(F32)
    y = cb_ref[...] + cw_ref[0:1, :] * xbuf[5:5 + L, :]
    for j in range(1, M_CONV):
        y = y + cw_ref[j:j + 1, :] * xbuf[5 + j:5 + j + L, :]
    xbuf[0:8, :] = xbuf[L:L + 8, :]
    y = y * _sigmoid(y)

    rows = lax.broadcasted_iota(jnp.int32, (L, L), 0)
    cols = lax.broadcasted_iota(jnp.int32, (L, L), 1)
    causal = rows >= cols
    lane = lax.broadcasted_iota(jnp.int32, (L, M_DV), 1)
    ones_blk = jnp.where(lane == 0, 1.0, 0.0).astype(BF16)

    for h in range(M_HEADS):
        q = y[:, h * M_DQK:(h + 1) * M_DQK].astype(BF16)
        kf = y[:, M_QK_W + h * M_DQK:M_QK_W + (h + 1) * M_DQK] * (M_DQK ** -0.5)
        k = kf.astype(BF16)
        vext = jnp.concatenate([v_ref[:, h * M_DV:(h + 1) * M_DV], ones_blk], axis=-1)
        icol = gcol_ref[:, G_IPRE + h:G_IPRE + h + 1]
        bcol = gcol_ref[:, G_BM + h:G_BM + h + 1]
        rowterm = irow_ref[h:h + 1, :] - brow_ref[h:h + 1, :]
        bprev = bend_s[h:h + 1, 0:1]
        mprev = m_s[h:h + 1, 0:1]

        dmat = jnp.where(causal, bcol + rowterm, -jnp.inf)
        inter = (bcol - bprev) + mprev
        m_t = jnp.maximum(inter, jnp.max(dmat, axis=1, keepdims=True))
        w_intra = jnp.exp(dmat - m_t)
        w_inter = jnp.exp(inter - m_t)
        s = lax.dot_general(q, k, (((1,), (1,)), ((), ())), preferred_element_type=F32) * w_intra
        cx = c_s[h]
        nd = (jnp.dot(s.astype(BF16), vext, preferred_element_type=F32)
              + w_inter * jnp.dot(q, cx.astype(BF16), preferred_element_type=F32))
        num = nd[:, :M_DV]
        den = nd[:, M_DV:M_DV + 1]
        hv = num / jnp.maximum(jnp.abs(den), jnp.exp(-m_t))
        ms = jnp.mean(hv * hv, axis=-1, keepdims=True)
        hn = hv * lax.rsqrt(ms + EPS) * gn_ref[:, h * M_DV:(h + 1) * M_DV]
        og = og_ref[:, h * M_DV:(h + 1) * M_DV].astype(F32)
        out_ref[:, h * M_DV:(h + 1) * M_DV] = (hn * _sigmoid(og)).astype(BF16)

        bend = bcol[L - 1:L, :]
        gtot = bend - bprev
        acol = (bend - bcol) + icol
        m_new = jnp.maximum(gtot + mprev, jnp.max(acol, axis=0, keepdims=True))
        wk = jnp.exp(acol - m_new)
        decay = jnp.exp(gtot + mprev - m_new)
        kw = (kf * wk).astype(BF16)
        upd = lax.dot_general(kw, vext, (((0,), (0,)), ((), ())), preferred_element_type=F32)
        c_s[h] = decay * cx + upd
        m_s[h:h + 1, :] = jnp.broadcast_to(m_new, (1, LANES))
        bend_s[h:h + 1, :] = jnp.broadcast_to(bend, (1, LANES))


def _mlstm(z, gcol, grow, conv_w, conv_b, gnorm):
    s = z.shape[0]
    L = MLSTM_L
    const = lambda c: (0, 0)
    return pl.pallas_call(
        _mlstm_kernel,
        out_shape=jax.ShapeDtypeStruct((s, M_V_W), BF16),
        grid=(s // L,),
        in_specs=[
            pl.BlockSpec((L, PK_BLOCK), lambda c: (c, BLK_MQK)),
            pl.BlockSpec((L, PK_BLOCK), lambda c: (c, BLK_MV)),
            pl.BlockSpec((L, PK_BLOCK), lambda c: (c, BLK_MO)),
            pl.BlockSpec((L, LANES), lambda c: (c, 0)),
            pl.BlockSpec((M_HEADS, L), lambda c: (G_IPRE // M_HEADS, c)),
            pl.BlockSpec((M_HEADS, L), lambda c: (G_BM // M_HEADS, c)),
            pl.BlockSpec((M_CONV, 2 * M_QK_W), const),
            pl.BlockSpec((1, 2 * M_QK_W), const),
            pl.BlockSpec((1, M_V_W), const),
        ],
        out_specs=pl.BlockSpec((L, M_V_W), lambda c: (c, 0)),
        scratch_shapes=[
            pltpu.VMEM((L + 8, 2 * M_QK_W), F32),
            pltpu.VMEM((M_HEADS, M_DQK, 2 * M_DV), F32),
            pltpu.VMEM((M_HEADS, LANES), F32),
            pltpu.VMEM((M_HEADS, LANES), F32),
        ],
        compiler_params=_cparams(("arbitrary",)),
        name="mlstm",
    )(z, z, z, gcol, grow, grow, conv_w, conv_b, gnorm)


FOX_KX = 2 * F_DH
FOX_NPIECE = 3
FOX_VT = F_DH + 16
FOX_VT_ONE = F_DH


FOX_MASKED = -1e30


def _fox_kernel(q_ref, k_ref, v_ref, f_ref, og_ref, o_ref, kx_s, vt_s, s0, s1, p0, p1, acc_s):
    i = pl.program_id(1)
    TQ = q_ref.shape[0]
    T = FOX_TK
    S = k_ref.shape[0]

    @pl.when(i == 0)
    def _():
        lane = lax.broadcasted_iota(jnp.int32, (T, F_DH), 1)
        row = lax.broadcasted_iota(jnp.int32, (FOX_VT - F_DH, T), 0)
        ones_rows = jnp.where(row == 0, 1.0, 0.0).astype(BF16)

        def prep(c, carry):
            off = pl.multiple_of(c * T, T)
            fsel = jnp.where(lane == G_FF + pl.program_id(0), f_ref[pl.ds(off, T), :], 0.0)
            rem = -LOG2E * jnp.sum(fsel, axis=1, keepdims=True)
            ext = jnp.zeros((T, F_DH), F32)
            for piece in range(FOX_NPIECE):
                part = rem.astype(BF16).astype(F32)
                ext = jnp.where(lane == piece, part, ext)
                rem = rem - part
            kx_s[pl.ds(off, T), 0:F_DH] = k_ref[pl.ds(off, T), :]
            kx_s[pl.ds(off, T), F_DH:FOX_KX] = ext.astype(BF16)
            vt_s[0:F_DH, pl.ds(off, T)] = v_ref[pl.ds(off, T), :].astype(F32).T.astype(BF16)
            vt_s[F_DH:FOX_VT, pl.ds(off, T)] = ones_rows
            return carry

        lax.fori_loop(0, S // T, prep, 0)

    lane_q = lax.broadcasted_iota(jnp.int32, (TQ, F_DH), 1)
    qx = jnp.concatenate(
        [q_ref[...], jnp.where(lane_q < FOX_NPIECE, 1.0, 0.0).astype(BF16)], axis=1)

    def score(blk, s_buf, mask):
        off = pl.multiple_of(blk * T, T)
        st = lax.dot_general(kx_s[pl.ds(off, T), :], qx, (((1,), (1,)), ((), ())),
                             preferred_element_type=F32)
        if mask is not None:
            st = jnp.where(mask, st, FOX_MASKED)
        s_buf[...] = st
        return jnp.max(st, axis=0, keepdims=True)

    def soft(s_buf, p_buf, m, bmax):
        m_new = jnp.maximum(m, bmax)
        p_buf[...] = jnp.exp2(s_buf[...] - m_new).astype(BF16)
        return m_new, jnp.exp2(m - m_new)

    def accum(blk, p_buf, alpha):
        vt = vt_s[:, pl.ds(pl.multiple_of(blk * T, T), T)]
        acc_s[...] = alpha * acc_s[...] + jnp.dot(vt, p_buf[...], preferred_element_type=F32)

    keys = lax.broadcasted_iota(jnp.int32, (T, TQ), 0)
    queries = lax.broadcasted_iota(jnp.int32, (T, TQ), 1)
    acc_s[...] = jnp.zeros(acc_s.shape, F32)
    bmax = score(2 * i, s0, queries >= keys)
    m, alpha = soft(s0, p0, jnp.full((1, TQ), FOX_MASKED, F32), bmax)
    bmax = score(2 * i + 1, s1, queries >= keys + T)

    def body(d, carry):
        m, alpha, bmax = carry
        accum(jnp.where(d == 0, 2 * i, 2 * d - 2), p0, alpha)
        m, alpha = soft(s1, p1, m, bmax)
        bmax = score(2 * d, s0, None)
        accum(jnp.where(d == 0, 2 * i + 1, 2 * d - 1), p1, alpha)
        m, alpha = soft(s0, p0, m, bmax)
        bmax = score(2 * d + 1, s1, None)
        return m, alpha, bmax

    m, alpha, bmax = lax.fori_loop(0, i, body, (m, alpha, bmax))
    accum(jnp.where(i == 0, 0, 2 * i - 2), p0, alpha)
    m, alpha = soft(s1, p1, m, bmax)
    accum(jnp.where(i == 0, 1, 2 * i - 1), p1, alpha)
    acc = acc_s[...]
    out_t = acc[0:F_DH, :] / acc[FOX_VT_ONE:FOX_VT_ONE + 1, :]
    o_ref[...] = (out_t.T * _sigmoid(og_ref[...].astype(F32))).astype(BF16)


def _fox(z, fcum):
    s = z.shape[0]
    T, TQ = FOX_TK, FOX_TQ
    cpb = PK_BLOCK // F_DH
    return pl.pallas_call(
        _fox_kernel,
        out_shape=jax.ShapeDtypeStruct((s, F_W), BF16),
        grid=(F_HEADS, s // TQ),
        in_specs=[
            pl.BlockSpec((TQ, F_DH), lambda h, i: (i, BLK_FQ * cpb + h)),
            pl.BlockSpec((s, F_DH), lambda h, i: (0, BLK_FK * cpb + h)),
            pl.BlockSpec((s, F_DH), lambda h, i: (0, BLK_FV * cpb + h)),
            pl.BlockSpec((s, LANES), lambda h, i: (0, 0)),
            pl.BlockSpec((TQ, F_DH), lambda h, i: (i, BLK_FO * cpb + h)),
        ],
        out_specs=pl.BlockSpec((TQ, F_DH), lambda h, i: (i, h)),
        scratch_shapes=[pltpu.VMEM((s, FOX_KX), BF16), pltpu.VMEM((FOX_VT, s), BF16),
                        pltpu.VMEM((T, TQ), F32), pltpu.VMEM((T, TQ), F32),
                        pltpu.VMEM((T, TQ), BF16), pltpu.VMEM((T, TQ), BF16),
                        pltpu.VMEM((FOX_VT, TQ), F32)],
        compiler_params=_cparams(("arbitrary", "arbitrary")),
        name="fox_attention",
    )(z, z, z, fcum, z)


def _outproj_kernel(hm_ref, hf_ref, ga0_ref, ga1_ref, gb0_ref, gb1_ref, x_ref, wm_ref, wf_ref, wo_ref,
                    g1_ref, n2_ref, sc2_ref, sh2_ref, wr_ref, br_ref,
                    x1_ref, h2_ref, lg_ref):
    ym = jnp.dot(hm_ref[...], wm_ref[...], preferred_element_type=F32)
    yf = jnp.dot(hf_ref[...], wf_ref[...], preferred_element_type=F32)
    ga = jnp.concatenate([ga0_ref[...], ga1_ref[...]], axis=1).astype(F32)
    gb = jnp.concatenate([gb0_ref[...], gb1_ref[...]], axis=1).astype(F32)
    y = _sigmoid(ga) * ym + _sigmoid(gb) * yf
    mix = jnp.dot(y.astype(BF16), wo_ref[...], preferred_element_type=F32)
    x1 = x_ref[...] + g1_ref[...] * mix
    x1_ref[...] = x1
    hn = x1 * lax.rsqrt(jnp.mean(x1 * x1, axis=-1, keepdims=True) + EPS) * n2_ref[...]
    h2 = hn * (1.0 + sc2_ref[...]) + sh2_ref[...]
    h2_ref[...] = h2
    lg_ref[...] = _dot_split(h2, wr_ref) + br_ref[...]


def _outproj(hm, hf, z, x, wm, wf, wo, g1, n2, sc2, sh2, wr, br):
    s, d = x.shape
    tm = OUT_TM
    const = lambda m: (0, 0)
    return pl.pallas_call(
        _outproj_kernel,
        out_shape=(jax.ShapeDtypeStruct((s, d), F32),
                   jax.ShapeDtypeStruct((s, d), F32),
                   jax.ShapeDtypeStruct((s, LANES), F32)),
        grid=(s // tm,),
        in_specs=[
            pl.BlockSpec((tm, M_V_W), lambda m: (m, 0)),
            pl.BlockSpec((tm, F_W), lambda m: (m, 0)),
            pl.BlockSpec((tm, PK_BLOCK), lambda m: (m, BLK_GA)),
            pl.BlockSpec((tm, PK_BLOCK), lambda m: (m, BLK_GA + 1)),
            pl.BlockSpec((tm, PK_BLOCK), lambda m: (m, BLK_GB)),
            pl.BlockSpec((tm, PK_BLOCK), lambda m: (m, BLK_GB + 1)),
            pl.BlockSpec((tm, d), lambda m: (m, 0)),
            pl.BlockSpec((M_V_W, d), const, pipeline_mode=pl.Buffered(1)),
            pl.BlockSpec((F_W, d), const, pipeline_mode=pl.Buffered(1)),
            pl.BlockSpec((d, d), const, pipeline_mode=pl.Buffered(1)),
            pl.BlockSpec((1, d), const),
            pl.BlockSpec((1, d), const),
            pl.BlockSpec((1, d), const),
            pl.BlockSpec((1, d), const),
            pl.BlockSpec((d, 2 * LANES), const),
            pl.BlockSpec((1, LANES), const),
        ],
        out_specs=(pl.BlockSpec((tm, d), lambda m: (m, 0)),
                   pl.BlockSpec((tm, d), lambda m: (m, 0)),
                   pl.BlockSpec((tm, LANES), lambda m: (m, 0))),
        compiler_params=_cparams(("arbitrary",)),
        name="outproj",
    )(hm, hf, z, z, z, z, x, wm, wf, wo, g1, n2, sc2, sh2, wr, br)


def _group_tables(tile_e, n_used):
    nt = tile_e.shape[0]
    idx = jnp.arange(nt, dtype=jnp.int32)
    used = idx < n_used[0]
    prev = jnp.concatenate([tile_e[:1] - 1, tile_e[:-1]])
    first = jnp.logical_and(used, tile_e != prev).astype(jnp.int32)
    gseq = jnp.cumsum(first) - 1
    own = jnp.logical_and(first[None, :] == 1, gseq[None, :] == idx[:, None])
    gexp = jnp.sum(jnp.where(own, tile_e[None, :], 0), axis=1)
    return gseq.astype(jnp.int32), gexp.astype(jnp.int32), jnp.sum(first).reshape(1)


def _stream_group_weights(i, chunk, n_chunks, gseq_ref, ng_ref, copies_of):
    ng = ng_ref[0]
    g = chunk * ng + gseq_ref[i]
    slot = g % 2
    first = jnp.logical_or(i == 0, gseq_ref[i] != gseq_ref[jnp.maximum(i - 1, 0)])

    @pl.when(first)
    def _():
        @pl.when(g == 0)
        def _():
            for cp in copies_of(g, slot):
                cp.start()

        @pl.when(g + 1 < n_chunks * ng)
        def _():
            for cp in copies_of(g + 1, 1 - slot):
                cp.start()

        for cp in copies_of(g, slot):
            cp.wait()

    return slot


def _moe_gu_kernel(te_ref, nu_ref, gseq_ref, gexp_ref, ng_ref, x_ref, w_hbm, bg_ref, bu_ref,
                   o_ref, wbuf, sem):
    f = pl.program_id(0)
    i = pl.program_id(1)

    def copies_of(g, slot):
        fg = g // ng_ref[0]
        e = gexp_ref[g - fg * ng_ref[0]]
        col = pl.multiple_of(fg * MOE_TF, MOE_TF)
        return (pltpu.make_async_copy(w_hbm.at[e, :, pl.ds(col, MOE_TF)],
                                      wbuf.at[slot, 0], sem.at[slot]),
                pltpu.make_async_copy(w_hbm.at[e, :, pl.ds(D_FF + col, MOE_TF)],
                                      wbuf.at[slot, 1], sem.at[slot]))

    @pl.when(i < nu_ref[0])
    def _():
        slot = _stream_group_weights(i, f, pl.num_programs(0), gseq_ref, ng_ref, copies_of)
        x = x_ref[...].astype(BF16)
        g = jnp.dot(x, wbuf[slot, 0].astype(BF16), preferred_element_type=F32) + bg_ref[0]
        u = jnp.dot(x, wbuf[slot, 1].astype(BF16), preferred_element_type=F32) + bu_ref[0]
        gate = jnp.minimum(g, SWIGLU_LIMIT)
        up = jnp.clip(u, -SWIGLU_LIMIT, SWIGLU_LIMIT)
        act = (up + 1.0) * gate * _sigmoid(SWIGLU_ALPHA * gate)
        o_ref[...] = act.astype(BF16)

    @pl.when(i >= nu_ref[0])
    def _():
        o_ref[...] = jnp.zeros(o_ref.shape, BF16)


def _moe_gate_up(tables, xs, w_gu, b_gu):
    r, d = xs.shape
    nf = D_FF // MOE_TF
    grid_spec = pltpu.PrefetchScalarGridSpec(
        num_scalar_prefetch=len(tables),
        grid=(nf, r // MOE_TM),
        in_specs=[
            pl.BlockSpec((MOE_TM, d), lambda f, i, te, nu, *_: (jnp.minimum(i, nu[0] - 1), 0)),
            pl.BlockSpec(memory_space=pl.ANY),
            pl.BlockSpec((1, 1, MOE_TF), lambda f, i, te, *_: (te[i], 0, f)),
            pl.BlockSpec((1, 1, MOE_TF), lambda f, i, te, *_: (te[i], 0, nf + f)),
        ],
        out_specs=pl.BlockSpec((MOE_TM, MOE_TF), lambda f, i, *_: (i, f)),
        scratch_shapes=[pltpu.VMEM((2, 2, d, MOE_TF), F32), pltpu.SemaphoreType.DMA((2,))],
    )
    return pl.pallas_call(
        _moe_gu_kernel,
        out_shape=jax.ShapeDtypeStruct((r, D_FF), BF16),
        grid_spec=grid_spec,
        compiler_params=_cparams(("arbitrary", "arbitrary")),
        name="moe_gate_up",
    )(*tables, xs, w_gu, b_gu, b_gu)


def _moe_down_kernel(te_ref, nu_ref, gseq_ref, gexp_ref, ng_ref, a_ref, w_hbm, bd_ref, o_ref,
                     wbuf, sem):
    n = pl.program_id(0)
    i = pl.program_id(1)

    def copies_of(g, slot):
        ch = g // ng_ref[0]
        e = gexp_ref[g - ch * ng_ref[0]]
        col = pl.multiple_of(ch * MOE_TN, MOE_TN)
        return (pltpu.make_async_copy(w_hbm.at[e, :, pl.ds(col, MOE_TN)],
                                      wbuf.at[slot], sem.at[slot]),)

    @pl.when(i < nu_ref[0])
    def _():
        slot = _stream_group_weights(i, n, pl.num_programs(0), gseq_ref, ng_ref, copies_of)
        o_ref[...] = (jnp.dot(a_ref[...], wbuf[slot].astype(BF16), preferred_element_type=F32)
                      + bd_ref[0])

    @pl.when(i >= nu_ref[0])
    def _():
        o_ref[...] = jnp.zeros(o_ref.shape, F32)


def _moe_down(tables, act, w_d, b_d):
    r = act.shape[0]
    nn = D_MODEL // MOE_TN
    grid_spec = pltpu.PrefetchScalarGridSpec(
        num_scalar_prefetch=len(tables),
        grid=(nn, r // MOE_TM),
        in_specs=[
            pl.BlockSpec((MOE_TM, D_FF), lambda n, i, te, nu, *_: (jnp.minimum(i, nu[0] - 1), 0)),
            pl.BlockSpec(memory_space=pl.ANY),
            pl.BlockSpec((1, 1, MOE_TN), lambda n, i, te, *_: (te[i], 0, n)),
        ],
        out_specs=pl.BlockSpec((MOE_TM, MOE_TN), lambda n, i, *_: (i, n)),
        scratch_shapes=[pltpu.VMEM((2, D_FF, MOE_TN), F32), pltpu.SemaphoreType.DMA((2,))],
    )
    return pl.pallas_call(
        _moe_down_kernel,
        out_shape=jax.ShapeDtypeStruct((r, D_MODEL), F32),
        grid_spec=grid_spec,
        compiler_params=_cparams(("arbitrary", "arbitrary")),
        name="moe_down",
    )(*tables, act, w_d, b_d)


def _final_kernel(pos_ref, x1_ref, w_ref, g2_ref, gf_ref, y_hbm, o_ref, buf, sem):
    m = pl.program_id(0)
    nm = pl.num_programs(0)
    tm = x1_ref.shape[0]

    def issue(step, slot):
        base = step * (tm * TOP_K)

        def body(t, carry):
            for k in range(TOP_K):
                r = pos_ref[base + t * TOP_K + k]
                pltpu.make_async_copy(y_hbm.at[pl.ds(r, 1), :],
                                      buf.at[slot, k, pl.ds(t, 1), :], sem.at[slot]).start()
            return carry

        lax.fori_loop(0, tm, body, 0, unroll=DMA_ISSUE_UNROLL)

    @pl.when(m == 0)
    def _():
        issue(0, 0)

    @pl.when(m + 1 < nm)
    def _():
        issue(m + 1, (m + 1) % 2)

    slot = m % 2
    for k in range(TOP_K):
        pltpu.make_async_copy(y_hbm.at[pl.ds(0, tm), :], buf.at[slot, k], sem.at[slot]).wait()
    w = w_ref[...]
    moe = w[:, 0:1] * buf[slot, 0]
    for k in range(1, TOP_K):
        moe = moe + w[:, k:k + 1] * buf[slot, k]
    x2 = x1_ref[...] + g2_ref[...] * moe
    o_ref[...] = x2 * lax.rsqrt(jnp.mean(x2 * x2, axis=-1, keepdims=True) + EPS) * gf_ref[...]


def _final(pos_flat, x1, w_top, g2, gf, ys):
    s, d = x1.shape
    tm = FIN_TM
    const = lambda m, pos: (0, 0)
    grid_spec = pltpu.PrefetchScalarGridSpec(
        num_scalar_prefetch=1,
        grid=(s // tm,),
        in_specs=[
            pl.BlockSpec((tm, d), lambda m, pos: (m, 0)),
            pl.BlockSpec((tm, LANES), lambda m, pos: (m, 0)),
            pl.BlockSpec((1, d), const),
            pl.BlockSpec((1, d), const),
            pl.BlockSpec(memory_space=pl.ANY),
        ],
        out_specs=pl.BlockSpec((tm, d), lambda m, pos: (m, 0)),
        scratch_shapes=[pltpu.VMEM((2, TOP_K, tm, d), F32), pltpu.SemaphoreType.DMA((2,))],
    )
    return pl.pallas_call(
        _final_kernel,
        out_shape=jax.ShapeDtypeStruct((s, d), F32),
        grid_spec=grid_spec,
        compiler_params=_cparams(("arbitrary",)),
        name="final_combine",
    )(pos_flat, x1, w_top, g2, gf, ys)


ROUTE_TM = 512
META_NUSED, META_PAD0, META_PADN = 0, 1, 2


def _route_kernel(lg_ref, pos_ref, w_ref, te_ref, nu_ref, cnt_s, pstart_s, carry_s):
    p = pl.program_id(0)
    m = pl.program_id(1)
    tm = lg_ref.shape[0]
    lane = lax.broadcasted_iota(jnp.int32, (tm, LANES), 1)
    lanef = lane.astype(F32)
    l = jnp.where(lane < N_EXPERTS, lg_ref[...], -jnp.inf)
    sel = jnp.zeros((tm, LANES), F32)
    vals, hots = [], []
    for _ in range(TOP_K):
        mx = jnp.max(l, axis=1, keepdims=True)
        idx = jnp.min(jnp.where(l == mx, lanef, float(LANES)), axis=1, keepdims=True)
        hot = lanef == idx
        vals.append(mx)
        hots.append(hot)
        l = jnp.where(hot, -jnp.inf, l)
        sel = sel + jnp.where(hot, 1.0, 0.0)
    colsum = jnp.sum(sel, axis=0, keepdims=True)

    @pl.when(p == 0)
    def _():
        @pl.when(m == 0)
        def _():
            cnt_s[...] = jnp.zeros(cnt_s.shape, F32)

        cnt_s[...] = cnt_s[...] + colsum

    @pl.when(jnp.logical_and(p == 1, m == 0))
    def _():
        cnt = cnt_s[...]
        padded = jnp.floor((cnt + (MOE_TM - 1.0)) * (1.0 / MOE_TM)) * MOE_TM
        lane8 = lax.broadcasted_iota(jnp.int32, cnt.shape, 1)
        pend = padded
        d = 1
        while d < LANES:
            pend = pend + jnp.where(lane8 >= d, pltpu.roll(pend, d, axis=1), 0.0)
            d *= 2
        pstart_s[...] = pend - padded
        carry_s[...] = jnp.zeros(carry_s.shape, F32)
        total = pend[:, N_EXPERTS - 1:N_EXPERTS]
        nt = te_ref.shape[0]
        tstart = lax.broadcasted_iota(jnp.int32, (nt, LANES), 0).astype(F32) * MOE_TM
        lane_t = lax.broadcasted_iota(jnp.int32, (nt, LANES), 1)
        done = jnp.where(lane_t < N_EXPERTS, jnp.where(tstart >= pend[0:1, :], 1.0, 0.0), 0.0)
        te = jnp.sum(done, axis=1, keepdims=True)
        last = jnp.sum(jnp.where(lane8 < N_EXPERTS,
                                 jnp.where(pend <= total - 1.0, 1.0, 0.0), 0.0),
                       axis=1, keepdims=True)
        te_ref[...] = jnp.minimum(te, last[0:1, :]).astype(jnp.int32)
        mrow = lax.broadcasted_iota(jnp.int32, cnt.shape, 0)
        meta = jnp.where(mrow == META_NUSED, jnp.broadcast_to(total * (1.0 / MOE_TM), cnt.shape),
                         jnp.where(mrow == META_PAD0, pend - padded + cnt, padded - cnt))
        nu_ref[...] = meta.astype(jnp.int32)

    @pl.when(p == 1)
    def _():
        r = lax.broadcasted_iota(jnp.int32, (tm, tm), 0)
        c = lax.broadcasted_iota(jnp.int32, (tm, tm), 1)
        tri = jnp.where(r > c, 1.0, 0.0).astype(BF16)
        rowbase = (jnp.dot(tri, sel.astype(BF16), preferred_element_type=F32)
                   + carry_s[0:1, :] + pstart_s[0:1, :])
        carry_s[...] = carry_s[...] + colsum
        den = jnp.exp(vals[0] - vals[0])
        for k in range(1, TOP_K):
            den = den + jnp.exp(vals[k] - vals[0])
        posf = jnp.zeros((tm, LANES), F32)
        wf = jnp.zeros((tm, LANES), F32)
        for k in range(TOP_K):
            pk = jnp.sum(jnp.where(hots[k], rowbase, 0.0), axis=1, keepdims=True)
            wk = jnp.exp(vals[k] - vals[0]) / den
            posf = jnp.where(lane == k, pk, posf)
            wf = jnp.where(lane == k, wk, wf)
        pos_ref[...] = posf.astype(jnp.int32)
        w_ref[...] = wf


def _route(lg, n_tiles):
    s = lg.shape[0]
    tm = ROUTE_TM
    nt_pad = -(-n_tiles // 8) * 8
    const = lambda p, m: (0, 0)
    return pl.pallas_call(
        _route_kernel,
        out_shape=(jax.ShapeDtypeStruct((s, LANES), jnp.int32),
                   jax.ShapeDtypeStruct((s, LANES), F32),
                   jax.ShapeDtypeStruct((nt_pad, 1), jnp.int32),
                   jax.ShapeDtypeStruct((8, LANES), jnp.int32)),
        grid=(2, s // tm),
        in_specs=[pl.BlockSpec((tm, LANES), lambda p, m: (m, 0))],
        out_specs=(pl.BlockSpec((tm, LANES), lambda p, m: (m * p, 0)),
                   pl.BlockSpec((tm, LANES), lambda p, m: (m * p, 0)),
                   pl.BlockSpec((nt_pad, 1), const),
                   pl.BlockSpec((8, LANES), const)),
        scratch_shapes=[pltpu.VMEM((8, LANES), F32), pltpu.VMEM((8, LANES), F32),
                        pltpu.VMEM((8, LANES), F32)],
        compiler_params=_cparams(("arbitrary", "arbitrary")),
        name="route",
    )(lg)


DISP_TM = 256


ZERO_ROWS = MOE_TM // 2


def _dispatch_kernel(pos_ref, pad0_ref, padn_ref, nu_ref, h_ref, xs_hbm, zero_s, sem, zsem):
    m = pl.program_id(0)
    tm = h_ref.shape[0]
    base = m * (tm * TOP_K)
    n_tiles = xs_hbm.shape[0] // MOE_TM

    def pad_copies(e, act):
        start = pad0_ref[e]
        n = padn_ref[e]
        head = jnp.minimum(n, (-start) & (SUBLANES - 1))

        def one_row(r, c):
            act(pltpu.make_async_copy(zero_s.at[pl.ds(0, 1), :],
                                      xs_hbm.at[pl.ds(start + r, 1), :], zsem))
            return c

        lax.fori_loop(0, head, one_row, 0)
        start = start + head
        n = n - head
        b = ZERO_ROWS
        while b >= SUBLANES:
            @pl.when((n & b) != 0)
            def _(b=b):
                off = pl.multiple_of(start + (n & ~(2 * b - 1)), SUBLANES)
                act(pltpu.make_async_copy(zero_s.at[pl.ds(0, b), :],
                                          xs_hbm.at[pl.ds(off, b), :], zsem))
            b //= 2

    def tail_copies(j, act):
        for half in range(MOE_TM // ZERO_ROWS):
            off = pl.multiple_of(j * MOE_TM + half * ZERO_ROWS, ZERO_ROWS)
            act(pltpu.make_async_copy(zero_s, xs_hbm.at[pl.ds(off, ZERO_ROWS), :], zsem))

    @pl.when(m == 0)
    def _():
        zero_s[...] = jnp.zeros(zero_s.shape, F32)
        for act in (lambda cp: cp.start(), lambda cp: cp.wait()):
            lax.fori_loop(0, N_EXPERTS, lambda e, c, act=act: (pad_copies(e, act), c)[1], 0)
            lax.fori_loop(nu_ref[0], n_tiles, lambda j, c, act=act: (tail_copies(j, act), c)[1], 0)

    def body(t, carry):
        for k in range(TOP_K):
            r = pos_ref[base + t * TOP_K + k]
            pltpu.make_async_copy(h_ref.at[pl.ds(t, 1), :], xs_hbm.at[pl.ds(r, 1), :], sem).start()
        return carry

    lax.fori_loop(0, tm, body, 0, unroll=DMA_ISSUE_UNROLL)
    for _ in range(TOP_K):
        pltpu.make_async_copy(h_ref, xs_hbm.at[pl.ds(0, tm), :], sem).wait()


def _dispatch(pos_flat, pad0, padn, n_used, h2, n_rows):
    s, d = h2.shape
    tm = DISP_TM
    grid_spec = pltpu.PrefetchScalarGridSpec(
        num_scalar_prefetch=4,
        grid=(s // tm,),
        in_specs=[pl.BlockSpec((tm, d), lambda m, *_: (m, 0))],
        out_specs=pl.BlockSpec(memory_space=pl.ANY),
        scratch_shapes=[pltpu.VMEM((ZERO_ROWS, d), F32), pltpu.SemaphoreType.DMA(()),
                        pltpu.SemaphoreType.DMA(())],
    )
    return pl.pallas_call(
        _dispatch_kernel,
        out_shape=jax.ShapeDtypeStruct((n_rows, d), F32),
        grid_spec=grid_spec,
        compiler_params=_cparams(("arbitrary",)),
        name="dispatch",
    )(pos_flat, pad0, padn, n_used, h2)


def _gate_weights_t(w_in_t):
    return jnp.concatenate([w_in_t[W_IN_IGATE:W_IN_IGATE + 2 * M_HEADS],
                            w_in_t[W_IN_FGATE:W_IN_FGATE + F_HEADS],
                            jnp.zeros((LANES - G_ROWS, w_in_t.shape[1]), F32)], axis=0)


def kernel(x, c, ada_w, ada_b, norm1_g, w_in, m_conv_w, m_conv_b, m_i_bias, m_f_bias,
           m_out_norm_g, f_f_bias, f_q_norm_g, f_k_norm_g, w_branch_m, w_branch_f, w_out,
           norm2_g, w_router, b_router, w_gate_up, b_gate_up, w_down, b_down, final_norm_g):
    b, s, d = x.shape
    assert b == 1 and ada_w.shape[0] == 1
    x2d = x.reshape(s, d)

    mod = _ada_mod(c.reshape(d, 1), ada_w.reshape(d, 6 * d), ada_b.reshape(1, 6 * d))
    sh1, sc1, g1, sh2, sc2, g2 = [mod[:, i * d:(i + 1) * d] for i in range(6)]

    w_in_t = w_in.reshape(d, -1).T
    wg2 = jnp.concatenate(_split_bf16(_gate_weights_t(w_in_t)), axis=0)
    h1, zg = _norm1(x2d, norm1_g.reshape(1, d), sc1, sh1, wg2)
    z = _inproj(h1, w_in_t, f_q_norm_g.reshape(1, F_W), f_k_norm_g.reshape(1, F_W))

    gate_bias = jnp.concatenate([m_i_bias.reshape(1, M_HEADS), m_f_bias.reshape(1, M_HEADS),
                                 f_f_bias.reshape(1, F_HEADS),
                                 jnp.zeros((1, LANES - G_ROWS), F32)], axis=1)
    gcol, grow = _gates(zg, gate_bias)

    hm = _mlstm(z, gcol, grow, m_conv_w.reshape(M_CONV, 2 * M_QK_W),
                m_conv_b.reshape(1, 2 * M_QK_W), m_out_norm_g.reshape(1, M_V_W))
    hf = _fox(z, gcol)

    wr = jnp.concatenate([w_router.reshape(d, N_EXPERTS),
                          jnp.zeros((d, LANES - N_EXPERTS), F32)], axis=1)
    br = jnp.concatenate([b_router.reshape(1, N_EXPERTS),
                          jnp.zeros((1, LANES - N_EXPERTS), F32)], axis=1)
    x1, h2, lg = _outproj(hm, hf, z, x2d,
                          w_branch_m.reshape(M_V_W, d).astype(BF16),
                          w_branch_f.reshape(F_W, d).astype(BF16),
                          w_out.reshape(d, d).astype(BF16),
                          g1, norm2_g.reshape(1, d), sc2, sh2,
                          jnp.concatenate(_split_bf16(wr), axis=1), br)

    n_rows = s * TOP_K + N_EXPERTS * MOE_TM
    n_tiles = n_rows // MOE_TM
    pos, w_top, te, meta = _route(lg, n_tiles)
    pos_flat = pos[:, :TOP_K].reshape(s * TOP_K)
    tile_e = te[:n_tiles, 0]
    n_used = meta[META_NUSED, :1]
    xs = _dispatch(pos_flat, meta[META_PAD0, :N_EXPERTS], meta[META_PADN, :N_EXPERTS], n_used,
                   h2, n_rows)
    tables = (tile_e, n_used) + _group_tables(tile_e, n_used)
    act = _moe_gate_up(tables, xs, w_gate_up.reshape(N_EXPERTS, d, 2 * D_FF),
                       b_gate_up.reshape(N_EXPERTS, 1, 2 * D_FF))
    ys = _moe_down(tables, act, w_down.reshape(N_EXPERTS, D_FF, d),
                   b_down.reshape(N_EXPERTS, 1, d))
    out = _final(pos_flat, x1, w_top, g2, final_norm_g.reshape(1, d), ys)
    return out.reshape(b, s, d)
```

```python
import functools

import jax
import jax.numpy as jnp
from jax import lax
from jax.experimental import pallas as pl
from jax.experimental.pallas import tpu as pltpu

F32 = jnp.float32
BF16 = jnp.bfloat16

D_MODEL = 2048
M_HEADS = 8
M_DQK = 64
M_DV = 128
M_CONV = 4
F_HEADS = 8
F_DH = 128
N_EXPERTS = 32
TOP_K = 4
D_FF = 2048
SWIGLU_LIMIT = 7.0
SWIGLU_ALPHA = 1.702
EPS = 1e-6
LOG2E = 1.4426950408889634

M_QK_W = M_HEADS * M_DQK
M_V_W = M_HEADS * M_DV
F_W = F_HEADS * F_DH

VMEM_LIMIT = 56 * 1024 * 1024
LANES = 128
SUBLANES = 8

PK_BLOCK = 1024
PK_NBLK = 11
BLK_MQK, BLK_MV, BLK_MO, BLK_FQ, BLK_FK, BLK_FV, BLK_FO, BLK_GA, BLK_GB = 0, 1, 2, 3, 4, 5, 6, 7, 9
W_IN_IGATE = 2 * M_QK_W + M_V_W
W_IN_FGATE = W_IN_IGATE + 2 * M_HEADS + M_V_W + 3 * F_W
PK_SHIFT_GROUPS = ((0, BLK_MQK, BLK_MO), (2 * M_HEADS, BLK_MO, BLK_FO),
                   (2 * M_HEADS + F_HEADS, BLK_FO, PK_NBLK))

NORM_TM = 512
IN_TM = 1024
MLSTM_L = 256
FOX_TK = 512
FOX_TQ = 2 * FOX_TK
OUT_TM = 256
MOE_TM = 256
MOE_TF = 1024
MOE_TN = 1024
FIN_TM = 256
DMA_ISSUE_UNROLL = 4


def _cparams(sem):
    return pltpu.CompilerParams(dimension_semantics=sem, vmem_limit_bytes=VMEM_LIMIT)


def _sigmoid(x):
    return 1.0 / (1.0 + jnp.exp(-x))


def _split_bf16(a):
    c = a * 65537.0
    hi = c - (c - a)
    return hi.astype(BF16), (a - hi).astype(BF16)


def _dot_split(a, w2_ref, transposed=False):
    a_hi, a_lo = _split_bf16(a)
    if transposed:
        nt = (((1,), (1,)), ((), ()))
        zz = lax.dot_general(a_hi, w2_ref[...], nt, preferred_element_type=F32)
        lo = lax.dot_general(a_lo, w2_ref[:LANES, :], nt, preferred_element_type=F32)
    else:
        zz = jnp.dot(a_hi, w2_ref[...], preferred_element_type=F32)
        lo = jnp.dot(a_lo, w2_ref[:, :LANES], preferred_element_type=F32)
    return zz[:, :LANES] + zz[:, LANES:] + lo


ADA_TN = 1024


def _ada_kernel(c_ref, w_ref, b_ref, o_ref):
    nrow = w_ref.shape[0] // 8

    def body(r, acc):
        rows = pl.ds(pl.multiple_of(r * 8, 8), 8)
        c = c_ref[rows, :]
        sc = c * _sigmoid(c)
        return acc + sc * w_ref[rows, :]

    acc = lax.fori_loop(0, nrow, body, jnp.zeros((8, w_ref.shape[1]), F32), unroll=8)
    o_ref[...] = jnp.sum(acc, axis=0, keepdims=True) + b_ref[...]


def _ada_mod(c_col, ada_w, ada_b):
    d, n = ada_w.shape
    return pl.pallas_call(
        _ada_kernel,
        out_shape=jax.ShapeDtypeStruct((1, n), F32),
        grid=(n // ADA_TN,),
        in_specs=[
            pl.BlockSpec((d, 1), lambda j: (0, 0)),
            pl.BlockSpec((d, ADA_TN), lambda j: (0, j)),
            pl.BlockSpec((1, ADA_TN), lambda j: (0, j)),
        ],
        out_specs=pl.BlockSpec((1, ADA_TN), lambda j: (0, j)),
        compiler_params=_cparams(("arbitrary",)),
        name="ada_mod",
    )(c_col, ada_w, ada_b)


def _head_rmsnorm(a, g_row, scale):
    outs = []
    for h in range(F_HEADS):
        ah = a[:, h * F_DH:(h + 1) * F_DH]
        ms = jnp.mean(ah * ah, axis=-1, keepdims=True)
        outs.append(ah * lax.rsqrt(ms + EPS) * (g_row[:, h * F_DH:(h + 1) * F_DH] * scale))
    return jnp.concatenate(outs, axis=-1)


def _norm1_kernel(x_ref, n1_ref, sc1_ref, sh1_ref, wg_ref, h_ref, zg_ref):
    x = x_ref[...]
    y = x * lax.rsqrt(jnp.mean(x * x, axis=-1, keepdims=True) + EPS) * n1_ref[...]
    h = y * (1.0 + sc1_ref[...]) + sh1_ref[...]
    h_ref[...] = h.astype(BF16)
    zg_ref[...] = _dot_split(h, wg_ref, transposed=True)


def _norm1(x, n1, sc1, sh1, w_gates):
    s, d = x.shape
    tm = NORM_TM
    row = lambda m: (0, 0)
    return pl.pallas_call(
        _norm1_kernel,
        out_shape=(jax.ShapeDtypeStruct((s, d), BF16), jax.ShapeDtypeStruct((s, LANES), F32)),
        grid=(s // tm,),
        in_specs=[pl.BlockSpec((tm, d), lambda m: (m, 0)), pl.BlockSpec((1, d), row),
                  pl.BlockSpec((1, d), row), pl.BlockSpec((1, d), row),
                  pl.BlockSpec((2 * LANES, d), row)],
        out_specs=(pl.BlockSpec((tm, d), lambda m: (m, 0)),
                   pl.BlockSpec((tm, LANES), lambda m: (m, 0))),
        compiler_params=_cparams(("arbitrary",)),
        name="norm1",
    )(x, n1, sc1, sh1, w_gates)


def _inproj_kernel(h_ref, wa_ref, wb_ref, qg_ref, kg_ref, z_ref, w_s):
    n = pl.program_id(0)
    m = pl.program_id(1)

    for shift, lo, hi in PK_SHIFT_GROUPS:
        @pl.when(jnp.logical_and(m == 0, jnp.logical_and(n >= lo, n < hi)))
        def _(shift=shift):
            if shift == 0:
                w_s[...] = wa_ref[...].astype(BF16)
            else:
                w_s[...] = jnp.concatenate([wa_ref[shift:, :], wb_ref[:shift, :]],
                                           axis=0).astype(BF16)

    acc = lax.dot_general(h_ref[...], w_s[...], (((1,), (1,)), ((), ())),
                          preferred_element_type=F32)

    @pl.when(n == BLK_FQ)
    def _():
        z_ref[...] = _head_rmsnorm(acc, qg_ref[...], F_DH ** -0.5 * LOG2E).astype(BF16)

    @pl.when(n == BLK_FK)
    def _():
        z_ref[...] = _head_rmsnorm(acc, kg_ref[...], 1.0).astype(BF16)

    @pl.when(jnp.logical_and(n != BLK_FQ, n != BLK_FK))
    def _():
        z_ref[...] = acc.astype(BF16)


def _inproj(h, w_in, qg, kg):
    s, d = h.shape
    row = lambda n, m: (0, 0)
    lpb = PK_BLOCK // LANES
    return pl.pallas_call(
        _inproj_kernel,
        out_shape=jax.ShapeDtypeStruct((s, PK_NBLK * PK_BLOCK), BF16),
        grid=(PK_NBLK, s // IN_TM),
        in_specs=[
            pl.BlockSpec((IN_TM, d), lambda n, m: (m, 0)),
            pl.BlockSpec((PK_BLOCK, d), lambda n, m: (n, 0)),
            pl.BlockSpec((LANES, d), lambda n, m: (lpb * (n + 1), 0)),
            pl.BlockSpec((1, PK_BLOCK), row),
            pl.BlockSpec((1, PK_BLOCK), row),
        ],
        out_specs=pl.BlockSpec((IN_TM, PK_BLOCK), lambda n, m: (m, n)),
        scratch_shapes=[pltpu.VMEM((PK_BLOCK, d), BF16)],
        compiler_params=_cparams(("arbitrary", "arbitrary")),
        name="inproj",
    )(h, w_in, w_in, qg, kg)


G_IPRE, G_BM, G_FF = 0, M_HEADS, 2 * M_HEADS
G_ROWS = 3 * M_HEADS
GATE_TM = 512


def _gates_kernel(zg_ref, b_ref, gcol_ref, grow_ref, carry_s):
    t = pl.program_id(0)
    tm = zg_ref.shape[0]

    @pl.when(t == 0)
    def _():
        carry_s[...] = jnp.zeros(carry_s.shape, F32)

    a = zg_ref[...] + b_ref[...]
    lane = lax.broadcasted_iota(jnp.int32, a.shape, 1)
    logf = jnp.minimum(a, 0.0) - jnp.log1p(jnp.exp(-jnp.abs(a)))
    is_f = jnp.logical_and(lane >= G_BM, lane < G_ROWS)
    r = lax.broadcasted_iota(jnp.int32, (tm, tm), 0)
    c = lax.broadcasted_iota(jnp.int32, (tm, tm), 1)
    tri = jnp.where(r >= c, 1.0, 0.0)
    cum = jnp.dot(tri, jnp.where(is_f, logf, 0.0), preferred_element_type=F32,
                  precision=lax.Precision.HIGHEST) + carry_s[0:1, :]
    carry_s[...] = jnp.broadcast_to(cum[tm - 1:tm, :], carry_s.shape)
    g = jnp.where(lane < G_BM, a, cum)
    gcol_ref[...] = g
    grow_ref[...] = g.T[0:G_ROWS, :]


def _gates(zg, bias_row):
    s = zg.shape[0]
    tm = GATE_TM
    return pl.pallas_call(
        _gates_kernel,
        out_shape=(jax.ShapeDtypeStruct((s, LANES), F32),
                   jax.ShapeDtypeStruct((G_ROWS, s), F32)),
        grid=(s // tm,),
        in_specs=[pl.BlockSpec((tm, LANES), lambda t: (t, 0)),
                  pl.BlockSpec((1, LANES), lambda t: (0, 0))],
        out_specs=(pl.BlockSpec((tm, LANES), lambda t: (t, 0)),
                   pl.BlockSpec((G_ROWS, tm), lambda t: (0, t))),
        scratch_shapes=[pltpu.VMEM((8, LANES), F32)],
        compiler_params=_cparams(("arbitrary",)),
        name="gates",
    )(zg, bias_row)


def _mlstm_kernel(qk_ref, v_ref, og_ref, gcol_ref, irow_ref, brow_ref,
                  cw_ref, cb_ref, gn_ref, out_ref, xbuf, c_s, m_s, bend_s):
    c = pl.program_id(0)
    L = qk_ref.shape[0]

    @pl.when(c == 0)
    def _():
        xbuf[0:8, :] = jnp.zeros((8, xbuf.shape[1]), F32)
        c_s[...] = jnp.zeros(c_s.shape, F32)
        m_s[...] = jnp.zeros(m_s.shape, F32)
        bend_s[...] = jnp.zeros(bend_s.shape, F32)

    xbuf[8:8 + L, :] = qk_ref[...].astype(F32)
    y = cb_ref[...] + cw_ref[0:1, :] * xbuf[5:5 + L, :]
    for j in range(1, M_CONV):
        y = y + cw_ref[j:j + 1, :] * xbuf[5 + j:5 + j + L, :]
    xbuf[0:8, :] = xbuf[L:L + 8, :]
    y = y * _sigmoid(y)

    yt = y.T
    src = lax.broadcasted_iota(jnp.int32, (L, L), 0)
    dst = lax.broadcasted_iota(jnp.int32, (L, L), 1)
    causal = src <= dst
    lane = lax.broadcasted_iota(jnp.int32, (L, M_DV), 1)
    ones_blk = jnp.where(lane == 0, 1.0, 0.0).astype(BF16)
    tn = (((0,), (0,)), ((), ()))

    for h in range(M_HEADS):
        qt = yt[h * M_DQK:(h + 1) * M_DQK, :].astype(BF16)
        ktf = yt[M_QK_W + h * M_DQK:M_QK_W + (h + 1) * M_DQK, :] * (M_DQK ** -0.5)
        vext = jnp.concatenate([v_ref[:, h * M_DV:(h + 1) * M_DV], ones_blk], axis=-1)
        colterm = (gcol_ref[:, G_IPRE + h:G_IPRE + h + 1]
                   - gcol_ref[:, G_BM + h:G_BM + h + 1])
        irow = irow_ref[h:h + 1, :]
        brow = brow_ref[h:h + 1, :]
        bprev = bend_s[h:h + 1, 0:1]
        mprev = m_s[h:h + 1, 0:1]

        dmat = jnp.where(causal, colterm + brow, -jnp.inf)
        inter = (brow - bprev) + mprev
        m_t = jnp.maximum(inter, jnp.max(dmat, axis=0, keepdims=True))
        w_inter = jnp.exp(inter - m_t)
        st = lax.dot_general(ktf.astype(BF16), qt, tn, preferred_element_type=F32)
        pt = (st * jnp.exp(dmat - m_t)).astype(BF16)
        cx = c_s[h]
        nd = (lax.dot_general(vext, pt, tn, preferred_element_type=F32)
              + w_inter * lax.dot_general(cx.astype(BF16), qt, tn, preferred_element_type=F32))
        den = nd[M_DV:M_DV + 1, :]
        hv = nd[0:M_DV, :] / jnp.maximum(jnp.abs(den), jnp.exp(-m_t))
        ms = jnp.mean(hv * hv, axis=0, keepdims=True)
        gain = jnp.concatenate([gn_ref[h * M_DV:(h + 1) * M_DV, :]] * (L // LANES), axis=1)
        hn = (hv * lax.rsqrt(ms + EPS) * gain).T
        og = og_ref[:, h * M_DV:(h + 1) * M_DV].astype(F32)
        out_ref[:, h * M_DV:(h + 1) * M_DV] = (hn * _sigmoid(og)).astype(BF16)

        bend = brow[:, L - 1:L]
        gtot = bend - bprev
        arow = (bend - brow) + irow
        m_new = jnp.maximum(gtot + mprev, jnp.max(arow, axis=1, keepdims=True))
        decay = jnp.exp(gtot + mprev - m_new)
        kwt = (ktf * jnp.exp(arow - m_new)).astype(BF16)
        c_s[h] = decay * cx + jnp.dot(kwt, vext, preferred_element_type=F32)
        m_s[h:h + 1, :] = jnp.broadcast_to(m_new, (1, LANES))
        bend_s[h:h + 1, :] = jnp.broadcast_to(bend, (1, LANES))


def _mlstm(z, gcol, grow, conv_w, conv_b, gnorm):
    s = z.shape[0]
    L = MLSTM_L
    const = lambda c: (0, 0)
    return pl.pallas_call(
        _mlstm_kernel,
        out_shape=jax.ShapeDtypeStruct((s, M_V_W), BF16),
        grid=(s // L,),
        in_specs=[
            pl.BlockSpec((L, PK_BLOCK), lambda c: (c, BLK_MQK)),
            pl.BlockSpec((L, PK_BLOCK), lambda c: (c, BLK_MV)),
            pl.BlockSpec((L, PK_BLOCK), lambda c: (c, BLK_MO)),
            pl.BlockSpec((L, LANES), lambda c: (c, 0)),
            pl.BlockSpec((M_HEADS, L), lambda c: (G_IPRE // M_HEADS, c)),
            pl.BlockSpec((M_HEADS, L), lambda c: (G_BM // M_HEADS, c)),
            pl.BlockSpec((M_CONV, 2 * M_QK_W), const),
            pl.BlockSpec((1, 2 * M_QK_W), const),
            pl.BlockSpec((M_V_W, LANES), const),
        ],
        out_specs=pl.BlockSpec((L, M_V_W), lambda c: (c, 0)),
        scratch_shapes=[
            pltpu.VMEM((L + 8, 2 * M_QK_W), F32),
            pltpu.VMEM((M_HEADS, M_DQK, 2 * M_DV), F32),
            pltpu.VMEM((M_HEADS, LANES), F32),
            pltpu.VMEM((M_HEADS, LANES), F32),
        ],
        compiler_params=_cparams(("arbitrary",)),
        name="mlstm",
    )(z, z, z, gcol, grow, grow, conv_w, conv_b, gnorm)


FOX_KX = 2 * F_DH
FOX_NPIECE = 3
FOX_VT = F_DH + 16
FOX_VT_ONE = F_DH


FOX_MASKED = -1e30


def _fox_kernel(q_ref, k_ref, v_ref, f_ref, og_ref, o_ref, kx_s, vt_s, s0, s1, p0, p1, acc_s):
    i = pl.program_id(1)
    TQ = q_ref.shape[0]
    T = FOX_TK
    S = k_ref.shape[0]

    @pl.when(i == 0)
    def _():
        lane = lax.broadcasted_iota(jnp.int32, (T, F_DH), 1)
        row = lax.broadcasted_iota(jnp.int32, (FOX_VT - F_DH, T), 0)
        ones_rows = jnp.where(row == 0, 1.0, 0.0).astype(BF16)

        def prep(c, carry):
            off = pl.multiple_of(c * T, T)
            fsel = jnp.where(lane == G_FF + pl.program_id(0), f_ref[pl.ds(off, T), :], 0.0)
            rem = -LOG2E * jnp.sum(fsel, axis=1, keepdims=True)
            ext = jnp.zeros((T, F_DH), F32)
            for piece in range(FOX_NPIECE):
                part = rem.astype(BF16).astype(F32)
                ext = jnp.where(lane == piece, part, ext)
                rem = rem - part
            kx_s[pl.ds(off, T), 0:F_DH] = k_ref[pl.ds(off, T), :]
            kx_s[pl.ds(off, T), F_DH:FOX_KX] = ext.astype(BF16)
            vt_s[0:F_DH, pl.ds(off, T)] = v_ref[pl.ds(off, T), :].astype(F32).T.astype(BF16)
            vt_s[F_DH:FOX_VT, pl.ds(off, T)] = ones_rows
            return carry

        lax.fori_loop(0, S // T, prep, 0)

    lane_q = lax.broadcasted_iota(jnp.int32, (TQ, F_DH), 1)
    qx = jnp.concatenate(
        [q_ref[...], jnp.where(lane_q < FOX_NPIECE, 1.0, 0.0).astype(BF16)], axis=1)

    def score(blk, s_buf, mask):
        off = pl.multiple_of(blk * T, T)
        st = lax.dot_general(kx_s[pl.ds(off, T), :], qx, (((1,), (1,)), ((), ())),
                             preferred_element_type=F32)
        if mask is not None:
            st = jnp.where(mask, st, FOX_MASKED)
        s_buf[...] = st
        return jnp.max(st, axis=0, keepdims=True)

    def soft(s_buf, p_buf, m, bmax):
        m_new = jnp.maximum(m, bmax)
        p_buf[...] = jnp.exp2(s_buf[...] - m_new).astype(BF16)
        return m_new, jnp.exp2(m - m_new)

    def accum(blk, p_buf, alpha):
        vt = vt_s[:, pl.ds(pl.multiple_of(blk * T, T), T)]
        acc_s[...] = alpha * acc_s[...] + jnp.dot(vt, p_buf[...], preferred_element_type=F32)

    keys = lax.broadcasted_iota(jnp.int32, (T, TQ), 0)
    queries = lax.broadcasted_iota(jnp.int32, (T, TQ), 1)
    acc_s[...] = jnp.zeros(acc_s.shape, F32)
    bmax = score(2 * i, s0, queries >= keys)
    m, alpha = soft(s0, p0, jnp.full((1, TQ), FOX_MASKED, F32), bmax)
    bmax = score(2 * i + 1, s1, queries >= keys + T)

    def body(d, carry):
        m, alpha, bmax = carry
        accum(jnp.where(d == 0, 2 * i, 2 * d - 2), p0, alpha)
        m, alpha = soft(s1, p1, m, bmax)
        bmax = score(2 * d, s0, None)
        accum(jnp.where(d == 0, 2 * i + 1, 2 * d - 1), p1, alpha)
        m, alpha = soft(s0, p0, m, bmax)
        bmax = score(2 * d + 1, s1, None)
        return m, alpha, bmax

    m, alpha, bmax = lax.fori_loop(0, i, body, (m, alpha, bmax))
    accum(jnp.where(i == 0, 0, 2 * i - 2), p0, alpha)
    m, alpha = soft(s1, p1, m, bmax)
    accum(jnp.where(i == 0, 1, 2 * i - 1), p1, alpha)
    acc = acc_s[...]
    out_t = acc[0:F_DH, :] / acc[FOX_VT_ONE:FOX_VT_ONE + 1, :]
    o_ref[...] = (out_t.T * _sigmoid(og_ref[...].astype(F32))).astype(BF16)


def _fox(z, fcum):
    s = z.shape[0]
    T, TQ = FOX_TK, FOX_TQ
    cpb = PK_BLOCK // F_DH
    return pl.pallas_call(
        _fox_kernel,
        out_shape=jax.ShapeDtypeStruct((s, F_W), BF16),
        grid=(F_HEADS, s // TQ),
        in_specs=[
            pl.BlockSpec((TQ, F_DH), lambda h, i: (i, BLK_FQ * cpb + h)),
            pl.BlockSpec((s, F_DH), lambda h, i: (0, BLK_FK * cpb + h)),
            pl.BlockSpec((s, F_DH), lambda h, i: (0, BLK_FV * cpb + h)),
            pl.BlockSpec((s, LANES), lambda h, i: (0, 0)),
            pl.BlockSpec((TQ, F_DH), lambda h, i: (i, BLK_FO * cpb + h)),
        ],
        out_specs=pl.BlockSpec((TQ, F_DH), lambda h, i: (i, h)),
        scratch_shapes=[pltpu.VMEM((s, FOX_KX), BF16), pltpu.VMEM((FOX_VT, s), BF16),
                        pltpu.VMEM((T, TQ), F32), pltpu.VMEM((T, TQ), F32),
                        pltpu.VMEM((T, TQ), BF16), pltpu.VMEM((T, TQ), BF16),
                        pltpu.VMEM((FOX_VT, TQ), F32)],
        compiler_params=_cparams(("arbitrary", "arbitrary")),
        name="fox_attention",
    )(z, z, z, fcum, z)


def _outproj_kernel(hm_ref, hf_ref, ga0_ref, ga1_ref, gb0_ref, gb1_ref, x_ref, wm_ref, wf_ref, wo_ref,
                    g1_ref, n2_ref, sc2_ref, sh2_ref, wr_ref, br_ref,
                    x1_ref, h2_ref, lg_ref):
    ym = jnp.dot(hm_ref[...], wm_ref[...], preferred_element_type=F32)
    yf = jnp.dot(hf_ref[...], wf_ref[...], preferred_element_type=F32)
    ga = jnp.concatenate([ga0_ref[...], ga1_ref[...]], axis=1).astype(F32)
    gb = jnp.concatenate([gb0_ref[...], gb1_ref[...]], axis=1).astype(F32)
    y = _sigmoid(ga) * ym + _sigmoid(gb) * yf
    mix = jnp.dot(y.astype(BF16), wo_ref[...], preferred_element_type=F32)
    x1 = x_ref[...] + g1_ref[...] * mix
    x1_ref[...] = x1
    hn = x1 * lax.rsqrt(jnp.mean(x1 * x1, axis=-1, keepdims=True) + EPS) * n2_ref[...]
    h2 = hn * (1.0 + sc2_ref[...]) + sh2_ref[...]
    h2_ref[...] = h2
    lg_ref[...] = _dot_split(h2, wr_ref) + br_ref[...]


def _outproj(hm, hf, z, x, wm, wf, wo, g1, n2, sc2, sh2, wr, br):
    s, d = x.shape
    tm = OUT_TM
    const = lambda m: (0, 0)
    return pl.pallas_call(
        _outproj_kernel,
        out_shape=(jax.ShapeDtypeStruct((s, d), F32),
                   jax.ShapeDtypeStruct((s, d), F32),
                   jax.ShapeDtypeStruct((s, LANES), F32)),
        grid=(s // tm,),
        in_specs=[
            pl.BlockSpec((tm, M_V_W), lambda m: (m, 0)),
            pl.BlockSpec((tm, F_W), lambda m: (m, 0)),
            pl.BlockSpec((tm, PK_BLOCK), lambda m: (m, BLK_GA)),
            pl.BlockSpec((tm, PK_BLOCK), lambda m: (m, BLK_GA + 1)),
            pl.BlockSpec((tm, PK_BLOCK), lambda m: (m, BLK_GB)),
            pl.BlockSpec((tm, PK_BLOCK), lambda m: (m, BLK_GB + 1)),
            pl.BlockSpec((tm, d), lambda m: (m, 0)),
            pl.BlockSpec((M_V_W, d), const, pipeline_mode=pl.Buffered(1)),
            pl.BlockSpec((F_W, d), const, pipeline_mode=pl.Buffered(1)),
            pl.BlockSpec((d, d), const, pipeline_mode=pl.Buffered(1)),
            pl.BlockSpec((1, d), const),
            pl.BlockSpec((1, d), const),
            pl.BlockSpec((1, d), const),
            pl.BlockSpec((1, d), const),
            pl.BlockSpec((d, 2 * LANES), const),
            pl.BlockSpec((1, LANES), const),
        ],
        out_specs=(pl.BlockSpec((tm, d), lambda m: (m, 0)),
                   pl.BlockSpec((tm, d), lambda m: (m, 0)),
                   pl.BlockSpec((tm, LANES), lambda m: (m, 0))),
        compiler_params=_cparams(("arbitrary",)),
        name="outproj",
    )(hm, hf, z, z, z, z, x, wm, wf, wo, g1, n2, sc2, sh2, wr, br)


def _group_tables(tile_e, n_used):
    nt = tile_e.shape[0]
    idx = jnp.arange(nt, dtype=jnp.int32)
    used = idx < n_used[0]
    prev = jnp.concatenate([tile_e[:1] - 1, tile_e[:-1]])
    first = jnp.logical_and(used, tile_e != prev).astype(jnp.int32)
    gseq = jnp.cumsum(first) - 1
    own = jnp.logical_and(first[None, :] == 1, gseq[None, :] == idx[:, None])
    gexp = jnp.sum(jnp.where(own, tile_e[None, :], 0), axis=1)
    return gseq.astype(jnp.int32), gexp.astype(jnp.int32), jnp.sum(first).reshape(1)


def _stream_group_weights(i, chunk, n_chunks, gseq_ref, ng_ref, copies_of):
    ng = ng_ref[0]
    g = chunk * ng + gseq_ref[i]
    slot = g % 2
    first = jnp.logical_or(i == 0, gseq_ref[i] != gseq_ref[jnp.maximum(i - 1, 0)])

    @pl.when(first)
    def _():
        @pl.when(g == 0)
        def _():
            for cp in copies_of(g, slot):
                cp.start()

        @pl.when(g + 1 < n_chunks * ng)
        def _():
            for cp in copies_of(g + 1, 1 - slot):
                cp.start()

        for cp in copies_of(g, slot):
            cp.wait()

    return slot


def _moe_gu_kernel(te_ref, nu_ref, gseq_ref, gexp_ref, ng_ref, x_ref, w_hbm, bg_ref, bu_ref,
                   o_ref, wbuf, sem):
    f = pl.program_id(0)
    i = pl.program_id(1)

    def copies_of(g, slot):
        fg = g // ng_ref[0]
        e = gexp_ref[g - fg * ng_ref[0]]
        col = pl.multiple_of(fg * MOE_TF, MOE_TF)
        return (pltpu.make_async_copy(w_hbm.at[e, :, pl.ds(col, MOE_TF)],
                                      wbuf.at[slot, 0], sem.at[slot]),
                pltpu.make_async_copy(w_hbm.at[e, :, pl.ds(D_FF + col, MOE_TF)],
                                      wbuf.at[slot, 1], sem.at[slot]))

    @pl.when(i < nu_ref[0])
    def _():
        slot = _stream_group_weights(i, f, pl.num_programs(0), gseq_ref, ng_ref, copies_of)
        x = x_ref[...].astype(BF16)
        g = jnp.dot(x, wbuf[slot, 0].astype(BF16), preferred_element_type=F32) + bg_ref[0]
        u = jnp.dot(x, wbuf[slot, 1].astype(BF16), preferred_element_type=F32) + bu_ref[0]
        gate = jnp.minimum(g, SWIGLU_LIMIT)
        up = jnp.clip(u, -SWIGLU_LIMIT, SWIGLU_LIMIT)
        act = (up + 1.0) * gate * _sigmoid(SWIGLU_ALPHA * gate)
        o_ref[...] = act.astype(BF16)

    @pl.when(i >= nu_ref[0])
    def _():
        o_ref[...] = jnp.zeros(o_ref.shape, BF16)


def _moe_gate_up(tables, xs, w_gu, b_gu):
    r, d = xs.shape
    nf = D_FF // MOE_TF
    grid_spec = pltpu.PrefetchScalarGridSpec(
        num_scalar_prefetch=len(tables),
        grid=(nf, r // MOE_TM),
        in_specs=[
            pl.BlockSpec((MOE_TM, d), lambda f, i, te, nu, *_: (jnp.minimum(i, nu[0] - 1), 0)),
            pl.BlockSpec(memory_space=pl.ANY),
            pl.BlockSpec((1, 1, MOE_TF), lambda f, i, te, *_: (te[i], 0, f)),
            pl.BlockSpec((1, 1, MOE_TF), lambda f, i, te, *_: (te[i], 0, nf + f)),
        ],
        out_specs=pl.BlockSpec((MOE_TM, MOE_TF), lambda f, i, *_: (i, f)),
        scratch_shapes=[pltpu.VMEM((2, 2, d, MOE_TF), F32), pltpu.SemaphoreType.DMA((2,))],
    )
    return pl.pallas_call(
        _moe_gu_kernel,
        out_shape=jax.ShapeDtypeStruct((r, D_FF), BF16),
        grid_spec=grid_spec,
        compiler_params=_cparams(("arbitrary", "arbitrary")),
        name="moe_gate_up",
    )(*tables, xs, w_gu, b_gu, b_gu)


def _moe_down_kernel(te_ref, nu_ref, gseq_ref, gexp_ref, ng_ref, a_ref, w_hbm, bd_ref, o_ref,
                     wbuf, sem):
    n = pl.program_id(0)
    i = pl.program_id(1)

    def copies_of(g, slot):
        ch = g // ng_ref[0]
        e = gexp_ref[g - ch * ng_ref[0]]
        col = pl.multiple_of(ch * MOE_TN, MOE_TN)
        return (pltpu.make_async_copy(w_hbm.at[e, :, pl.ds(col, MOE_TN)],
                                      wbuf.at[slot], sem.at[slot]),)

    @pl.when(i < nu_ref[0])
    def _():
        slot = _stream_group_weights(i, n, pl.num_programs(0), gseq_ref, ng_ref, copies_of)
        o_ref[...] = (jnp.dot(a_ref[...], wbuf[slot].astype(BF16), preferred_element_type=F32)
                      + bd_ref[0])

    @pl.when(i >= nu_ref[0])
    def _():
        o_ref[...] = jnp.zeros(o_ref.shape, F32)


def _moe_down(tables, act, w_d, b_d):
    r = act.shape[0]
    nn = D_MODEL // MOE_TN
    grid_spec = pltpu.PrefetchScalarGridSpec(
        num_scalar_prefetch=len(tables),
        grid=(nn, r // MOE_TM),
        in_specs=[
            pl.BlockSpec((MOE_TM, D_FF), lambda n, i, te, nu, *_: (jnp.minimum(i, nu[0] - 1), 0)),
            pl.BlockSpec(memory_space=pl.ANY),
            pl.BlockSpec((1, 1, MOE_TN), lambda n, i, te, *_: (te[i], 0, n)),
        ],
        out_specs=pl.BlockSpec((MOE_TM, MOE_TN), lambda n, i, *_: (i, n)),
        scratch_shapes=[pltpu.VMEM((2, D_FF, MOE_TN), F32), pltpu.SemaphoreType.DMA((2,))],
    )
    return pl.pallas_call(
        _moe_down_kernel,
        out_shape=jax.ShapeDtypeStruct((r, D_MODEL), F32),
        grid_spec=grid_spec,
        compiler_params=_cparams(("arbitrary", "arbitrary")),
        name="moe_down",
    )(*tables, act, w_d, b_d)


def _final_kernel(pos_ref, x1_ref, w_ref, g2_ref, gf_ref, y_hbm, o_ref, buf, sem):
    m = pl.program_id(0)
    nm = pl.num_programs(0)
    tm = x1_ref.shape[0]

    def issue(step, slot):
        base = step * (tm * TOP_K)

        def body(t, carry):
            for k in range(TOP_K):
                r = pos_ref[base + t * TOP_K + k]
                pltpu.make_async_copy(y_hbm.at[pl.ds(r, 1), :],
                                      buf.at[slot, k, pl.ds(t, 1), :], sem.at[slot]).start()
            return carry

        lax.fori_loop(0, tm, body, 0, unroll=DMA_ISSUE_UNROLL)

    @pl.when(m == 0)
    def _():
        issue(0, 0)

    @pl.when(m + 1 < nm)
    def _():
        issue(m + 1, (m + 1) % 2)

    slot = m % 2
    for k in range(TOP_K):
        pltpu.make_async_copy(y_hbm.at[pl.ds(0, tm), :], buf.at[slot, k], sem.at[slot]).wait()
    w = w_ref[...]
    moe = w[:, 0:1] * buf[slot, 0]
    for k in range(1, TOP_K):
        moe = moe + w[:, k:k + 1] * buf[slot, k]
    x2 = x1_ref[...] + g2_ref[...] * moe
    o_ref[...] = x2 * lax.rsqrt(jnp.mean(x2 * x2, axis=-1, keepdims=True) + EPS) * gf_ref[...]


def _final(pos_flat, x1, w_top, g2, gf, ys):
    s, d = x1.shape
    tm = FIN_TM
    const = lambda m, pos: (0, 0)
    grid_spec = pltpu.PrefetchScalarGridSpec(
        num_scalar_prefetch=1,
        grid=(s // tm,),
        in_specs=[
            pl.BlockSpec((tm, d), lambda m, pos: (m, 0)),
            pl.BlockSpec((tm, LANES), lambda m, pos: (m, 0)),
            pl.BlockSpec((1, d), const),
            pl.BlockSpec((1, d), const),
            pl.BlockSpec(memory_space=pl.ANY),
        ],
        out_specs=pl.BlockSpec((tm, d), lambda m, pos: (m, 0)),
        scratch_shapes=[pltpu.VMEM((2, TOP_K, tm, d), F32), pltpu.SemaphoreType.DMA((2,))],
    )
    return pl.pallas_call(
        _final_kernel,
        out_shape=jax.ShapeDtypeStruct((s, d), F32),
        grid_spec=grid_spec,
        compiler_params=_cparams(("arbitrary",)),
        name="final_combine",
    )(pos_flat, x1, w_top, g2, gf, ys)


ROUTE_TM = 512
META_NUSED, META_PAD0, META_PADN = 0, 1, 2


def _route_kernel(lg_ref, pos_ref, w_ref, te_ref, nu_ref, cnt_s, pstart_s, carry_s):
    p = pl.program_id(0)
    m = pl.program_id(1)
    tm = lg_ref.shape[0]
    lane = lax.broadcasted_iota(jnp.int32, (tm, LANES), 1)
    lanef = lane.astype(F32)
    l = jnp.where(lane < N_EXPERTS, lg_ref[...], -jnp.inf)
    sel = jnp.zeros((tm, LANES), F32)
    vals, hots = [], []
    for _ in range(TOP_K):
        mx = jnp.max(l, axis=1, keepdims=True)
        idx = jnp.min(jnp.where(l == mx, lanef, float(LANES)), axis=1, keepdims=True)
        hot = lanef == idx
        vals.append(mx)
        hots.append(hot)
        l = jnp.where(hot, -jnp.inf, l)
        sel = sel + jnp.where(hot, 1.0, 0.0)
    colsum = jnp.sum(sel, axis=0, keepdims=True)

    @pl.when(p == 0)
    def _():
        @pl.when(m == 0)
        def _():
            cnt_s[...] = jnp.zeros(cnt_s.shape, F32)

        cnt_s[...] = cnt_s[...] + colsum

    @pl.when(jnp.logical_and(p == 1, m == 0))
    def _():
        cnt = cnt_s[...]
        padded = jnp.floor((cnt + (MOE_TM - 1.0)) * (1.0 / MOE_TM)) * MOE_TM
        lane8 = lax.broadcasted_iota(jnp.int32, cnt.shape, 1)
        pend = padded
        d = 1
        while d < LANES:
            pend = pend + jnp.where(lane8 >= d, pltpu.roll(pend, d, axis=1), 0.0)
            d *= 2
        pstart_s[...] = pend - padded
        carry_s[...] = jnp.zeros(carry_s.shape, F32)
        total = pend[:, N_EXPERTS - 1:N_EXPERTS]
        nt = te_ref.shape[0]
        tstart = lax.broadcasted_iota(jnp.int32, (nt, LANES), 0).astype(F32) * MOE_TM
        lane_t = lax.broadcasted_iota(jnp.int32, (nt, LANES), 1)
        done = jnp.where(lane_t < N_EXPERTS, jnp.where(tstart >= pend[0:1, :], 1.0, 0.0), 0.0)
        te = jnp.sum(done, axis=1, keepdims=True)
        last = jnp.sum(jnp.where(lane8 < N_EXPERTS,
                                 jnp.where(pend <= total - 1.0, 1.0, 0.0), 0.0),
                       axis=1, keepdims=True)
        te_ref[...] = jnp.minimum(te, last[0:1, :]).astype(jnp.int32)
        mrow = lax.broadcasted_iota(jnp.int32, cnt.shape, 0)
        meta = jnp.where(mrow == META_NUSED, jnp.broadcast_to(total * (1.0 / MOE_TM), cnt.shape),
                         jnp.where(mrow == META_PAD0, pend - padded + cnt, padded - cnt))
        nu_ref[...] = meta.astype(jnp.int32)

    @pl.when(p == 1)
    def _():
        r = lax.broadcasted_iota(jnp.int32, (tm, tm), 0)
        c = lax.broadcasted_iota(jnp.int32, (tm, tm), 1)
        tri = jnp.where(r > c, 1.0, 0.0).astype(BF16)
        rowbase = (jnp.dot(tri, sel.astype(BF16), preferred_element_type=F32)
                   + carry_s[0:1, :] + pstart_s[0:1, :])
        carry_s[...] = carry_s[...] + colsum
        den = jnp.exp(vals[0] - vals[0])
        for k in range(1, TOP_K):
            den = den + jnp.exp(vals[k] - vals[0])
        posf = jnp.zeros((tm, LANES), F32)
        wf = jnp.zeros((tm, LANES), F32)
        for k in range(TOP_K):
            pk = jnp.sum(jnp.where(hots[k], rowbase, 0.0), axis=1, keepdims=True)
            wk = jnp.exp(vals[k] - vals[0]) / den
            posf = jnp.where(lane == k, pk, posf)
            wf = jnp.where(lane == k, wk, wf)
        pos_ref[...] = posf.astype(jnp.int32)
        w_ref[...] = wf


def _route(lg, n_tiles):
    s = lg.shape[0]
    tm = ROUTE_TM
    nt_pad = -(-n_tiles // 8) * 8
    const = lambda p, m: (0, 0)
    return pl.pallas_call(
        _route_kernel,
        out_shape=(jax.ShapeDtypeStruct((s, LANES), jnp.int32),
                   jax.ShapeDtypeStruct((s, LANES), F32),
                   jax.ShapeDtypeStruct((nt_pad, 1), jnp.int32),
                   jax.ShapeDtypeStruct((8, LANES), jnp.int32)),
        grid=(2, s // tm),
        in_specs=[pl.BlockSpec((tm, LANES), lambda p, m: (m, 0))],
        out_specs=(pl.BlockSpec((tm, LANES), lambda p, m: (m * p, 0)),
                   pl.BlockSpec((tm, LANES), lambda p, m: (m * p, 0)),
                   pl.BlockSpec((nt_pad, 1), const),
                   pl.BlockSpec((8, LANES), const)),
        scratch_shapes=[pltpu.VMEM((8, LANES), F32), pltpu.VMEM((8, LANES), F32),
                        pltpu.VMEM((8, LANES), F32)],
        compiler_params=_cparams(("arbitrary", "arbitrary")),
        name="route",
    )(lg)


DISP_TM = 256


ZERO_ROWS = MOE_TM // 2


def _dispatch_kernel(pos_ref, pad0_ref, padn_ref, nu_ref, h_ref, xs_hbm, zero_s, sem, zsem):
    m = pl.program_id(0)
    tm = h_ref.shape[0]
    base = m * (tm * TOP_K)
    n_tiles = xs_hbm.shape[0] // MOE_TM

    def pad_copies(e, act):
        start = pad0_ref[e]
        n = padn_ref[e]
        head = jnp.minimum(n, (-start) & (SUBLANES - 1))

        def one_row(r, c):
            act(pltpu.make_async_copy(zero_s.at[pl.ds(0, 1), :],
                                      xs_hbm.at[pl.ds(start + r, 1), :], zsem))
            return c

        lax.fori_loop(0, head, one_row, 0)
        start = start + head
        n = n - head
        b = ZERO_ROWS
        while b >= SUBLANES:
            @pl.when((n & b) != 0)
            def _(b=b):
                off = pl.multiple_of(start + (n & ~(2 * b - 1)), SUBLANES)
                act(pltpu.make_async_copy(zero_s.at[pl.ds(0, b), :],
                                          xs_hbm.at[pl.ds(off, b), :], zsem))
            b //= 2

    def tail_copies(j, act):
        for half in range(MOE_TM // ZERO_ROWS):
            off = pl.multiple_of(j * MOE_TM + half * ZERO_ROWS, ZERO_ROWS)
            act(pltpu.make_async_copy(zero_s, xs_hbm.at[pl.ds(off, ZERO_ROWS), :], zsem))

    @pl.when(m == 0)
    def _():
        zero_s[...] = jnp.zeros(zero_s.shape, F32)
        for act in (lambda cp: cp.start(), lambda cp: cp.wait()):
            lax.fori_loop(0, N_EXPERTS, lambda e, c, act=act: (pad_copies(e, act), c)[1], 0)
            lax.fori_loop(nu_ref[0], n_tiles, lambda j, c, act=act: (tail_copies(j, act), c)[1], 0)

    def body(t, carry):
        for k in range(TOP_K):
            r = pos_ref[base + t * TOP_K + k]
            pltpu.make_async_copy(h_ref.at[pl.ds(t, 1), :], xs_hbm.at[pl.ds(r, 1), :], sem).start()
        return carry

    lax.fori_loop(0, tm, body, 0, unroll=DMA_ISSUE_UNROLL)
    for _ in range(TOP_K):
        pltpu.make_async_copy(h_ref, xs_hbm.at[pl.ds(0, tm), :], sem).wait()


def _dispatch(pos_flat, pad0, padn, n_used, h2, n_rows):
    s, d = h2.shape
    tm = DISP_TM
    grid_spec = pltpu.PrefetchScalarGridSpec(
        num_scalar_prefetch=4,
        grid=(s // tm,),
        in_specs=[pl.BlockSpec((tm, d), lambda m, *_: (m, 0))],
        out_specs=pl.BlockSpec(memory_space=pl.ANY),
        scratch_shapes=[pltpu.VMEM((ZERO_ROWS, d), F32), pltpu.SemaphoreType.DMA(()),
                        pltpu.SemaphoreType.DMA(())],
    )
    return pl.pallas_call(
        _dispatch_kernel,
        out_shape=jax.ShapeDtypeStruct((n_rows, d), F32),
        grid_spec=grid_spec,
        compiler_params=_cparams(("arbitrary",)),
        name="dispatch",
    )(pos_flat, pad0, padn, n_used, h2)


def _gate_weights_t(w_in_t):
    return jnp.concatenate([w_in_t[W_IN_IGATE:W_IN_IGATE + 2 * M_HEADS],
                            w_in_t[W_IN_FGATE:W_IN_FGATE + F_HEADS],
                            jnp.zeros((LANES - G_ROWS, w_in_t.shape[1]), F32)], axis=0)


def kernel(x, c, ada_w, ada_b, norm1_g, w_in, m_conv_w, m_conv_b, m_i_bias, m_f_bias,
           m_out_norm_g, f_f_bias, f_q_norm_g, f_k_norm_g, w_branch_m, w_branch_f, w_out,
           norm2_g, w_router, b_router, w_gate_up, b_gate_up, w_down, b_down, final_norm_g):
    b, s, d = x.shape
    assert b == 1 and ada_w.shape[0] == 1
    x2d = x.reshape(s, d)

    mod = _ada_mod(c.reshape(d, 1), ada_w.reshape(d, 6 * d), ada_b.reshape(1, 6 * d))
    sh1, sc1, g1, sh2, sc2, g2 = [mod[:, i * d:(i + 1) * d] for i in range(6)]

    w_in_t = w_in.reshape(d, -1).T
    wg2 = jnp.concatenate(_split_bf16(_gate_weights_t(w_in_t)), axis=0)
    h1, zg = _norm1(x2d, norm1_g.reshape(1, d), sc1, sh1, wg2)
    z = _inproj(h1, w_in_t, f_q_norm_g.reshape(1, F_W), f_k_norm_g.reshape(1, F_W))

    gate_bias = jnp.concatenate([m_i_bias.reshape(1, M_HEADS), m_f_bias.reshape(1, M_HEADS),
                                 f_f_bias.reshape(1, F_HEADS),
                                 jnp.zeros((1, LANES - G_ROWS), F32)], axis=1)
    gcol, grow = _gates(zg, gate_bias)

    hm = _mlstm(z, gcol, grow, m_conv_w.reshape(M_CONV, 2 * M_QK_W),
                m_conv_b.reshape(1, 2 * M_QK_W),
                jnp.broadcast_to(m_out_norm_g.reshape(M_V_W, 1), (M_V_W, LANES)))
    hf = _fox(z, gcol)

    wr = jnp.concatenate([w_router.reshape(d, N_EXPERTS),
                          jnp.zeros((d, LANES - N_EXPERTS), F32)], axis=1)
    br = jnp.concatenate([b_router.reshape(1, N_EXPERTS),
                          jnp.zeros((1, LANES - N_EXPERTS), F32)], axis=1)
    x1, h2, lg = _outproj(hm, hf, z, x2d,
                          w_branch_m.reshape(M_V_W, d).astype(BF16),
                          w_branch_f.reshape(F_W, d).astype(BF16),
                          w_out.reshape(d, d).astype(BF16),
                          g1, norm2_g.reshape(1, d), sc2, sh2,
                          jnp.concatenate(_split_bf16(wr), axis=1), br)

    n_rows = s * TOP_K + N_EXPERTS * MOE_TM
    n_tiles = n_rows // MOE_TM
    pos, w_top, te, meta = _route(lg, n_tiles)
    pos_flat = pos[:, :TOP_K].reshape(s * TOP_K)
    tile_e = te[:n_tiles, 0]
    n_used = meta[META_NUSED, :1]
    xs = _dispatch(pos_flat, meta[META_PAD0, :N_EXPERTS], meta[META_PADN, :N_EXPERTS], n_used,
                   h2, n_rows)
    tables = (tile_e, n_used) + _group_tables(tile_e, n_used)
    act = _moe_gate_up(tables, xs, w_gate_up.reshape(N_EXPERTS, d, 2 * D_FF),
                       b_gate_up.reshape(N_EXPERTS, 1, 2 * D_FF))
    ys = _moe_down(tables, act, w_down.reshape(N_EXPERTS, D_FF, d),
                   b_down.reshape(N_EXPERTS, 1, d))
    out = _final(pos_flat, x1, w_top, g2, final_norm_g.reshape(1, d), ys)
    return out.reshape(b, s, d)
```

```python
import jax
import jax.numpy as jnp
from jax import lax
from jax.experimental import pallas as pl
from jax.experimental.pallas import tpu as pltpu

F32 = jnp.float32
BF16 = jnp.bfloat16

D_MODEL = 2048
M_HEADS = 8
M_DQK = 64
M_DV = 128
M_CONV = 4
F_HEADS = 8
F_DH = 128
N_EXPERTS = 32
TOP_K = 4
D_FF = 2048
SWIGLU_LIMIT = 7.0
SWIGLU_ALPHA = 1.702
EPS = 1e-6
LOG2E = 1.4426950408889634

M_QK_W = M_HEADS * M_DQK
M_V_W = M_HEADS * M_DV
F_W = F_HEADS * F_DH

VMEM_LIMIT = 56 * 1024 * 1024
LANES = 128
SUBLANES = 8

PK_BLOCK = 1024
PK_NBLK = 11
BLK_MQK, BLK_MV, BLK_MO, BLK_FQ, BLK_FK, BLK_FV, BLK_FO, BLK_GA, BLK_GB = 0, 1, 2, 3, 4, 5, 6, 7, 9
W_IN_IGATE = 2 * M_QK_W + M_V_W
W_IN_FGATE = W_IN_IGATE + 2 * M_HEADS + M_V_W + 3 * F_W
PK_SHIFT_GROUPS = ((0, BLK_MQK, BLK_MO), (2 * M_HEADS, BLK_MO, BLK_FO),
                   (2 * M_HEADS + F_HEADS, BLK_FO, PK_NBLK))

NORM_TM = 512
IN_TM = 1024
MLSTM_L = 256
FOX_TK = 512
FOX_TQ = 2 * FOX_TK
OUT_TM = 256
MOE_TM = 256
MOE_TF = 1024
MOE_TN = 2048
FIN_TM = 256
DMA_ISSUE_UNROLL = 4


def _cparams(sem):
    return pltpu.CompilerParams(dimension_semantics=sem, vmem_limit_bytes=VMEM_LIMIT)


def _sigmoid(x):
    return 1.0 / (1.0 + jnp.exp(-x))


def _split_bf16(a):
    c = a * 65537.0
    hi = c - (c - a)
    return hi.astype(BF16), (a - hi).astype(BF16)


def _dot_split(a, w2_ref, transposed=False):
    a_hi, a_lo = _split_bf16(a)
    if transposed:
        nt = (((1,), (1,)), ((), ()))
        zz = lax.dot_general(a_hi, w2_ref[...], nt, preferred_element_type=F32)
        lo = lax.dot_general(a_lo, w2_ref[:LANES, :], nt, preferred_element_type=F32)
    else:
        zz = jnp.dot(a_hi, w2_ref[...], preferred_element_type=F32)
        lo = jnp.dot(a_lo, w2_ref[:, :LANES], preferred_element_type=F32)
    return zz[:, :LANES] + zz[:, LANES:] + lo


ADA_TN = 1024


def _ada_kernel(c_ref, w_ref, b_ref, o_ref):
    nrow = w_ref.shape[0] // 8

    def body(r, acc):
        rows = pl.ds(pl.multiple_of(r * 8, 8), 8)
        c = c_ref[rows, :]
        sc = c * _sigmoid(c)
        return acc + sc * w_ref[rows, :]

    acc = lax.fori_loop(0, nrow, body, jnp.zeros((8, w_ref.shape[1]), F32), unroll=8)
    o_ref[...] = jnp.sum(acc, axis=0, keepdims=True) + b_ref[...]


def _ada_mod(c_col, ada_w, ada_b):
    d, n = ada_w.shape
    return pl.pallas_call(
        _ada_kernel,
        out_shape=jax.ShapeDtypeStruct((1, n), F32),
        grid=(n // ADA_TN,),
        in_specs=[
            pl.BlockSpec((d, 1), lambda j: (0, 0)),
            pl.BlockSpec((d, ADA_TN), lambda j: (0, j)),
            pl.BlockSpec((1, ADA_TN), lambda j: (0, j)),
        ],
        out_specs=pl.BlockSpec((1, ADA_TN), lambda j: (0, j)),
        compiler_params=_cparams(("arbitrary",)),
        name="ada_mod",
    )(c_col, ada_w, ada_b)


def _head_rmsnorm(a, g_row, scale):
    outs = []
    for h in range(F_HEADS):
        ah = a[:, h * F_DH:(h + 1) * F_DH]
        ms = jnp.mean(ah * ah, axis=-1, keepdims=True)
        outs.append(ah * lax.rsqrt(ms + EPS) * (g_row[:, h * F_DH:(h + 1) * F_DH] * scale))
    return jnp.concatenate(outs, axis=-1)


def _norm1_kernel(x_ref, n1_ref, sc1_ref, sh1_ref, wg_ref, h_ref, zg_ref):
    x = x_ref[...]
    y = x * lax.rsqrt(jnp.mean(x * x, axis=-1, keepdims=True) + EPS) * n1_ref[...]
    h = y * (1.0 + sc1_ref[...]) + sh1_ref[...]
    h_ref[...] = h.astype(BF16)
    zg_ref[...] = _dot_split(h, wg_ref, transposed=True)


def _norm1(x, n1, sc1, sh1, w_gates):
    s, d = x.shape
    tm = NORM_TM
    row = lambda m: (0, 0)
    return pl.pallas_call(
        _norm1_kernel,
        out_shape=(jax.ShapeDtypeStruct((s, d), BF16), jax.ShapeDtypeStruct((s, LANES), F32)),
        grid=(s // tm,),
        in_specs=[pl.BlockSpec((tm, d), lambda m: (m, 0)), pl.BlockSpec((1, d), row),
                  pl.BlockSpec((1, d), row), pl.BlockSpec((1, d), row),
                  pl.BlockSpec((2 * LANES, d), row)],
        out_specs=(pl.BlockSpec((tm, d), lambda m: (m, 0)),
                   pl.BlockSpec((tm, LANES), lambda m: (m, 0))),
        compiler_params=_cparams(("arbitrary",)),
        name="norm1",
    )(x, n1, sc1, sh1, w_gates)


def _inproj_kernel(h_ref, wa_ref, wb_ref, qg_ref, kg_ref, z_ref, w_s):
    n = pl.program_id(0)
    m = pl.program_id(1)

    for shift, lo, hi in PK_SHIFT_GROUPS:
        @pl.when(jnp.logical_and(m == 0, jnp.logical_and(n >= lo, n < hi)))
        def _(shift=shift):
            if shift == 0:
                w_s[...] = wa_ref[...].astype(BF16)
            else:
                w_s[...] = jnp.concatenate([wa_ref[shift:, :], wb_ref[:shift, :]],
                                           axis=0).astype(BF16)

    acc = lax.dot_general(h_ref[...], w_s[...], (((1,), (1,)), ((), ())),
                          preferred_element_type=F32)

    @pl.when(n == BLK_FQ)
    def _():
        z_ref[...] = _head_rmsnorm(acc, qg_ref[...], F_DH ** -0.5 * LOG2E).astype(BF16)

    @pl.when(n == BLK_FK)
    def _():
        z_ref[...] = _head_rmsnorm(acc, kg_ref[...], 1.0).astype(BF16)

    @pl.when(jnp.logical_and(n != BLK_FQ, n != BLK_FK))
    def _():
        z_ref[...] = acc.astype(BF16)


def _inproj(h, w_in, qg, kg):
    s, d = h.shape
    row = lambda n, m: (0, 0)
    lpb = PK_BLOCK // LANES
    return pl.pallas_call(
        _inproj_kernel,
        out_shape=jax.ShapeDtypeStruct((s, PK_NBLK * PK_BLOCK), BF16),
        grid=(PK_NBLK, s // IN_TM),
        in_specs=[
            pl.BlockSpec((IN_TM, d), lambda n, m: (m, 0)),
            pl.BlockSpec((PK_BLOCK, d), lambda n, m: (n, 0)),
            pl.BlockSpec((LANES, d), lambda n, m: (lpb * (n + 1), 0)),
            pl.BlockSpec((1, PK_BLOCK), row),
            pl.BlockSpec((1, PK_BLOCK), row),
        ],
        out_specs=pl.BlockSpec((IN_TM, PK_BLOCK), lambda n, m: (m, n)),
        scratch_shapes=[pltpu.VMEM((PK_BLOCK, d), BF16)],
        compiler_params=_cparams(("arbitrary", "arbitrary")),
        name="inproj",
    )(h, w_in, w_in, qg, kg)


G_IPRE, G_BM, G_FF = 0, M_HEADS, 2 * M_HEADS
G_ROWS = 3 * M_HEADS
GATE_TM = 512


def _gates_kernel(zg_ref, b_ref, gcol_ref, grow_ref, carry_s):
    t = pl.program_id(0)
    tm = zg_ref.shape[0]

    @pl.when(t == 0)
    def _():
        carry_s[...] = jnp.zeros(carry_s.shape, F32)

    a = zg_ref[...] + b_ref[...]
    lane = lax.broadcasted_iota(jnp.int32, a.shape, 1)
    logf = jnp.minimum(a, 0.0) - jnp.log1p(jnp.exp(-jnp.abs(a)))
    is_f = jnp.logical_and(lane >= G_BM, lane < G_ROWS)
    r = lax.broadcasted_iota(jnp.int32, (tm, tm), 0)
    c = lax.broadcasted_iota(jnp.int32, (tm, tm), 1)
    tri = jnp.where(r >= c, 1.0, 0.0)
    cum = jnp.dot(tri, jnp.where(is_f, logf, 0.0), preferred_element_type=F32,
                  precision=lax.Precision.HIGHEST) + carry_s[0:1, :]
    carry_s[...] = jnp.broadcast_to(cum[tm - 1:tm, :], carry_s.shape)
    g = jnp.where(lane < G_BM, a, cum)
    gcol_ref[...] = g
    grow_ref[...] = g.T[0:G_ROWS, :]


def _gates(zg, bias_row):
    s = zg.shape[0]
    tm = GATE_TM
    return pl.pallas_call(
        _gates_kernel,
        out_shape=(jax.ShapeDtypeStruct((s, LANES), F32),
                   jax.ShapeDtypeStruct((G_ROWS, s), F32)),
        grid=(s // tm,),
        in_specs=[pl.BlockSpec((tm, LANES), lambda t: (t, 0)),
                  pl.BlockSpec((1, LANES), lambda t: (0, 0))],
        out_specs=(pl.BlockSpec((tm, LANES), lambda t: (t, 0)),
                   pl.BlockSpec((G_ROWS, tm), lambda t: (0, t))),
        scratch_shapes=[pltpu.VMEM((8, LANES), F32)],
        compiler_params=_cparams(("arbitrary",)),
        name="gates",
    )(zg, bias_row)


def _mlstm_kernel(qk_ref, v_ref, og_ref, gcol_ref, irow_ref, brow_ref,
                  cw_ref, cb_ref, gn_ref, out_ref, xbuf, c_s, m_s, bend_s):
    c = pl.program_id(0)
    L = qk_ref.shape[0]

    @pl.when(c == 0)
    def _():
        xbuf[0:8, :] = jnp.zeros((8, xbuf.shape[1]), F32)
        c_s[...] = jnp.zeros(c_s.shape, F32)
        m_s[...] = jnp.zeros(m_s.shape, F32)
        bend_s[...] = jnp.zeros(bend_s.shape, F32)

    xbuf[8:8 + L, :] = qk_ref[...].astype(F32)
    y = cb_ref[...] + cw_ref[0:1, :] * xbuf[5:5 + L, :]
    for j in range(1, M_CONV):
        y = y + cw_ref[j:j + 1, :] * xbuf[5 + j:5 + j + L, :]
    xbuf[0:8, :] = xbuf[L:L + 8, :]
    y = y * _sigmoid(y)

    yt = y.T
    src = lax.broadcasted_iota(jnp.int32, (L, L), 0)
    dst = lax.broadcasted_iota(jnp.int32, (L, L), 1)
    causal = src <= dst
    lane = lax.broadcasted_iota(jnp.int32, (L, M_DV), 1)
    ones_blk = jnp.where(lane == 0, 1.0, 0.0).astype(BF16)
    tn = (((0,), (0,)), ((), ()))
    m_all, bend_all = m_s[...], bend_s[...]
    m_rows, bend_rows = [], []

    for h in range(M_HEADS):
        qt = yt[h * M_DQK:(h + 1) * M_DQK, :].astype(BF16)
        ktf = yt[M_QK_W + h * M_DQK:M_QK_W + (h + 1) * M_DQK, :] * (M_DQK ** -0.5)
        vext = jnp.concatenate([v_ref[:, h * M_DV:(h + 1) * M_DV], ones_blk], axis=-1)
        colterm = (gcol_ref[:, G_IPRE + h:G_IPRE + h + 1]
                   - gcol_ref[:, G_BM + h:G_BM + h + 1])
        irow = irow_ref[h:h + 1, :]
        brow = brow_ref[h:h + 1, :]
        bprev = bend_all[h:h + 1, 0:1]
        mprev = m_all[h:h + 1, 0:1]

        dmat = jnp.where(causal, colterm + brow, -jnp.inf)
        inter = (brow - bprev) + mprev
        m_t = jnp.maximum(inter, jnp.max(dmat, axis=0, keepdims=True))
        w_inter = jnp.exp(inter - m_t)
        st = lax.dot_general(ktf.astype(BF16), qt, tn, preferred_element_type=F32)
        pt = (st * jnp.exp(dmat - m_t)).astype(BF16)
        cx = c_s[h]
        nd = (lax.dot_general(vext, pt, tn, preferred_element_type=F32)
              + w_inter * lax.dot_general(cx.astype(BF16), qt, tn, preferred_element_type=F32))
        den = nd[M_DV:M_DV + 1, :]
        hv = nd[0:M_DV, :] / jnp.maximum(jnp.abs(den), jnp.exp(-m_t))
        ms = jnp.mean(hv * hv, axis=0, keepdims=True)
        gain = jnp.concatenate([gn_ref[h * M_DV:(h + 1) * M_DV, :]] * (L // LANES), axis=1)
        hn = (hv * lax.rsqrt(ms + EPS) * gain).T
        og = og_ref[:, h * M_DV:(h + 1) * M_DV].astype(F32)
        out_ref[:, h * M_DV:(h + 1) * M_DV] = (hn * _sigmoid(og)).astype(BF16)

        bend = brow[:, L - 1:L]
        gtot = bend - bprev
        arow = (bend - brow) + irow
        m_new = jnp.maximum(gtot + mprev, jnp.max(arow, axis=1, keepdims=True))
        decay = jnp.exp(gtot + mprev - m_new)
        kwt = (ktf * jnp.exp(arow - m_new)).astype(BF16)
        c_s[h] = decay * cx + jnp.dot(kwt, vext, preferred_element_type=F32)
        m_rows.append(jnp.broadcast_to(m_new, (1, LANES)))
        bend_rows.append(jnp.broadcast_to(bend, (1, LANES)))

    m_s[...] = jnp.concatenate(m_rows, axis=0)
    bend_s[...] = jnp.concatenate(bend_rows, axis=0)


def _mlstm(z, gcol, grow, conv_w, conv_b, gnorm):
    s = z.shape[0]
    L = MLSTM_L
    const = lambda c: (0, 0)
    return pl.pallas_call(
        _mlstm_kernel,
        out_shape=jax.ShapeDtypeStruct((s, M_V_W), BF16),
        grid=(s // L,),
        in_specs=[
            pl.BlockSpec((L, PK_BLOCK), lambda c: (c, BLK_MQK)),
            pl.BlockSpec((L, PK_BLOCK), lambda c: (c, BLK_MV)),
            pl.BlockSpec((L, PK_BLOCK), lambda c: (c, BLK_MO)),
            pl.BlockSpec((L, LANES), lambda c: (c, 0)),
            pl.BlockSpec((M_HEADS, L), lambda c: (G_IPRE // M_HEADS, c)),
            pl.BlockSpec((M_HEADS, L), lambda c: (G_BM // M_HEADS, c)),
            pl.BlockSpec((M_CONV, 2 * M_QK_W), const),
            pl.BlockSpec((1, 2 * M_QK_W), const),
            pl.BlockSpec((M_V_W, LANES), const),
        ],
        out_specs=pl.BlockSpec((L, M_V_W), lambda c: (c, 0)),
        scratch_shapes=[
            pltpu.VMEM((L + 8, 2 * M_QK_W), F32),
            pltpu.VMEM((M_HEADS, M_DQK, 2 * M_DV), F32),
            pltpu.VMEM((M_HEADS, LANES), F32),
            pltpu.VMEM((M_HEADS, LANES), F32),
        ],
        compiler_params=_cparams(("arbitrary",)),
        name="mlstm",
    )(z, z, z, gcol, grow, grow, conv_w, conv_b, gnorm)


FOX_KX = 2 * F_DH
FOX_NPIECE = 3
FOX_VT = F_DH + 16
FOX_VT_ONE = F_DH


FOX_MASKED = -1e30


def _fox_kernel(q_ref, k_ref, v_ref, f_ref, og_ref, o_ref, kx_s, vt_s, s0, s1, p0, p1, acc_s):
    i = pl.program_id(1)
    TQ = q_ref.shape[0]
    T = FOX_TK
    S = k_ref.shape[0]

    @pl.when(i == 0)
    def _():
        lane = lax.broadcasted_iota(jnp.int32, (T, F_DH), 1)
        row = lax.broadcasted_iota(jnp.int32, (FOX_VT - F_DH, T), 0)
        ones_rows = jnp.where(row == 0, 1.0, 0.0).astype(BF16)

        def prep(c, carry):
            off = pl.multiple_of(c * T, T)
            fsel = jnp.where(lane == G_FF + pl.program_id(0), f_ref[pl.ds(off, T), :], 0.0)
            rem = -LOG2E * jnp.sum(fsel, axis=1, keepdims=True)
            ext = jnp.zeros((T, F_DH), F32)
            for piece in range(FOX_NPIECE):
                part = rem.astype(BF16).astype(F32)
                ext = jnp.where(lane == piece, part, ext)
                rem = rem - part
            kx_s[pl.ds(off, T), 0:F_DH] = k_ref[pl.ds(off, T), :]
            kx_s[pl.ds(off, T), F_DH:FOX_KX] = ext.astype(BF16)
            vt_s[0:F_DH, pl.ds(off, T)] = v_ref[pl.ds(off, T), :].astype(F32).T.astype(BF16)
            vt_s[F_DH:FOX_VT, pl.ds(off, T)] = ones_rows
            return carry

        lax.fori_loop(0, S // T, prep, 0)

    lane_q = lax.broadcasted_iota(jnp.int32, (TQ, F_DH), 1)
    qx = jnp.concatenate(
        [q_ref[...], jnp.where(lane_q < FOX_NPIECE, 1.0, 0.0).astype(BF16)], axis=1)

    def score(blk, s_buf, mask):
        off = pl.multiple_of(blk * T, T)
        st = lax.dot_general(kx_s[pl.ds(off, T), :], qx, (((1,), (1,)), ((), ())),
                             preferred_element_type=F32)
        if mask is not None:
            st = jnp.where(mask, st, FOX_MASKED)
        s_buf[...] = st
        return jnp.max(st, axis=0, keepdims=True)

    def soft(s_buf, p_buf, m, bmax):
        m_new = jnp.maximum(m, bmax)
        p_buf[...] = jnp.exp2(s_buf[...] - m_new).astype(BF16)
        return m_new, jnp.exp2(m - m_new)

    def accum(blk, p_buf, alpha):
        vt = vt_s[:, pl.ds(pl.multiple_of(blk * T, T), T)]
        acc_s[...] = alpha * acc_s[...] + jnp.dot(vt, p_buf[...], preferred_element_type=F32)

    keys = lax.broadcasted_iota(jnp.int32, (T, TQ), 0)
    queries = lax.broadcasted_iota(jnp.int32, (T, TQ), 1)
    acc_s[...] = jnp.zeros(acc_s.shape, F32)
    bmax = score(2 * i, s0, queries >= keys)
    m, alpha = soft(s0, p0, jnp.full((1, TQ), FOX_MASKED, F32), bmax)
    bmax = score(2 * i + 1, s1, queries >= keys + T)

    def body(d, carry):
        m, alpha, bmax = carry
        accum(jnp.where(d == 0, 2 * i, 2 * d - 2), p0, alpha)
        m, alpha = soft(s1, p1, m, bmax)
        bmax = score(2 * d, s0, None)
        accum(jnp.where(d == 0, 2 * i + 1, 2 * d - 1), p1, alpha)
        m, alpha = soft(s0, p0, m, bmax)
        bmax = score(2 * d + 1, s1, None)
        return m, alpha, bmax

    m, alpha, bmax = lax.fori_loop(0, i, body, (m, alpha, bmax))
    accum(jnp.where(i == 0, 0, 2 * i - 2), p0, alpha)
    m, alpha = soft(s1, p1, m, bmax)
    accum(jnp.where(i == 0, 1, 2 * i - 1), p1, alpha)
    acc = acc_s[...]
    out_t = acc[0:F_DH, :] / acc[FOX_VT_ONE:FOX_VT_ONE + 1, :]
    o_ref[...] = (out_t.T * _sigmoid(og_ref[...].astype(F32))).astype(BF16)


def _fox(z, fcum):
    s = z.shape[0]
    T, TQ = FOX_TK, FOX_TQ
    cpb = PK_BLOCK // F_DH
    return pl.pallas_call(
        _fox_kernel,
        out_shape=jax.ShapeDtypeStruct((s, F_W), BF16),
        grid=(F_HEADS, s // TQ),
        in_specs=[
            pl.BlockSpec((TQ, F_DH), lambda h, i: (i, BLK_FQ * cpb + h)),
            pl.BlockSpec((s, F_DH), lambda h, i: (0, BLK_FK * cpb + h)),
            pl.BlockSpec((s, F_DH), lambda h, i: (0, BLK_FV * cpb + h)),
            pl.BlockSpec((s, LANES), lambda h, i: (0, 0)),
            pl.BlockSpec((TQ, F_DH), lambda h, i: (i, BLK_FO * cpb + h)),
        ],
        out_specs=pl.BlockSpec((TQ, F_DH), lambda h, i: (i, h)),
        scratch_shapes=[pltpu.VMEM((s, FOX_KX), BF16), pltpu.VMEM((FOX_VT, s), BF16),
                        pltpu.VMEM((T, TQ), F32), pltpu.VMEM((T, TQ), F32),
                        pltpu.VMEM((T, TQ), BF16), pltpu.VMEM((T, TQ), BF16),
                        pltpu.VMEM((FOX_VT, TQ), F32)],
        compiler_params=_cparams(("arbitrary", "arbitrary")),
        name="fox_attention",
    )(z, z, z, fcum, z)


def _outproj_kernel(hm_ref, hf_ref, ga0_ref, ga1_ref, gb0_ref, gb1_ref, x_ref, wm_ref, wf_ref, wo_ref,
                    g1_ref, n2_ref, sc2_ref, sh2_ref, wr_ref, br_ref,
                    x1_ref, h2_ref, lg_ref):
    ym = jnp.dot(hm_ref[...], wm_ref[...], preferred_element_type=F32)
    yf = jnp.dot(hf_ref[...], wf_ref[...], preferred_element_type=F32)
    ga = jnp.concatenate([ga0_ref[...], ga1_ref[...]], axis=1).astype(F32)
    gb = jnp.concatenate([gb0_ref[...], gb1_ref[...]], axis=1).astype(F32)
    y = _sigmoid(ga) * ym + _sigmoid(gb) * yf
    mix = jnp.dot(y.astype(BF16), wo_ref[...], preferred_element_type=F32)
    x1 = x_ref[...] + g1_ref[...] * mix
    x1_ref[...] = x1
    hn = x1 * lax.rsqrt(jnp.mean(x1 * x1, axis=-1, keepdims=True) + EPS) * n2_ref[...]
    h2 = hn * (1.0 + sc2_ref[...]) + sh2_ref[...]
    h2_ref[...] = h2
    lg_ref[...] = _dot_split(h2, wr_ref) + br_ref[...]


def _outproj(hm, hf, z, x, wm, wf, wo, g1, n2, sc2, sh2, wr, br):
    s, d = x.shape
    tm = OUT_TM
    const = lambda m: (0, 0)
    return pl.pallas_call(
        _outproj_kernel,
        out_shape=(jax.ShapeDtypeStruct((s, d), F32),
                   jax.ShapeDtypeStruct((s, d), F32),
                   jax.ShapeDtypeStruct((s, LANES), F32)),
        grid=(s // tm,),
        in_specs=[
            pl.BlockSpec((tm, M_V_W), lambda m: (m, 0)),
            pl.BlockSpec((tm, F_W), lambda m: (m, 0)),
            pl.BlockSpec((tm, PK_BLOCK), lambda m: (m, BLK_GA)),
            pl.BlockSpec((tm, PK_BLOCK), lambda m: (m, BLK_GA + 1)),
            pl.BlockSpec((tm, PK_BLOCK), lambda m: (m, BLK_GB)),
            pl.BlockSpec((tm, PK_BLOCK), lambda m: (m, BLK_GB + 1)),
            pl.BlockSpec((tm, d), lambda m: (m, 0)),
            pl.BlockSpec((M_V_W, d), const, pipeline_mode=pl.Buffered(1)),
            pl.BlockSpec((F_W, d), const, pipeline_mode=pl.Buffered(1)),
            pl.BlockSpec((d, d), const, pipeline_mode=pl.Buffered(1)),
            pl.BlockSpec((1, d), const),
            pl.BlockSpec((1, d), const),
            pl.BlockSpec((1, d), const),
            pl.BlockSpec((1, d), const),
            pl.BlockSpec((d, 2 * LANES), const),
            pl.BlockSpec((1, LANES), const),
        ],
        out_specs=(pl.BlockSpec((tm, d), lambda m: (m, 0)),
                   pl.BlockSpec((tm, d), lambda m: (m, 0)),
                   pl.BlockSpec((tm, LANES), lambda m: (m, 0))),
        compiler_params=_cparams(("arbitrary",)),
        name="outproj",
    )(hm, hf, z, z, z, z, x, wm, wf, wo, g1, n2, sc2, sh2, wr, br)


def _group_tables(tile_e, n_used):
    nt = tile_e.shape[0]
    idx = jnp.arange(nt, dtype=jnp.int32)
    used = idx < n_used[0]
    prev = jnp.concatenate([tile_e[:1] - 1, tile_e[:-1]])
    first = jnp.logical_and(used, tile_e != prev).astype(jnp.int32)
    gseq = jnp.cumsum(first) - 1
    own = jnp.logical_and(first[None, :] == 1, gseq[None, :] == idx[:, None])
    gexp = jnp.sum(jnp.where(own, tile_e[None, :], 0), axis=1)
    return gseq.astype(jnp.int32), gexp.astype(jnp.int32), jnp.sum(first).reshape(1)


def _stream_group_weights(i, chunk, n_chunks, gseq_ref, ng_ref, copies_of):
    ng = ng_ref[0]
    g = chunk * ng + gseq_ref[i]
    slot = g % 2
    first = jnp.logical_or(i == 0, gseq_ref[i] != gseq_ref[jnp.maximum(i - 1, 0)])

    @pl.when(first)
    def _():
        @pl.when(g == 0)
        def _():
            for cp in copies_of(g, slot):
                cp.start()

        @pl.when(g + 1 < n_chunks * ng)
        def _():
            for cp in copies_of(g + 1, 1 - slot):
                cp.start()

        for cp in copies_of(g, slot):
            cp.wait()

    return slot


def _moe_gu_kernel(te_ref, nu_ref, gseq_ref, gexp_ref, ng_ref, x_ref, w_hbm, bg_ref, bu_ref,
                   o_ref, wbuf, sem):
    f = pl.program_id(0)
    i = pl.program_id(1)

    def copies_of(g, slot):
        fg = g // ng_ref[0]
        e = gexp_ref[g - fg * ng_ref[0]]
        col = pl.multiple_of(fg * MOE_TF, MOE_TF)
        return (pltpu.make_async_copy(w_hbm.at[e, :, pl.ds(col, MOE_TF)],
                                      wbuf.at[slot, 0], sem.at[slot]),
                pltpu.make_async_copy(w_hbm.at[e, :, pl.ds(D_FF + col, MOE_TF)],
                                      wbuf.at[slot, 1], sem.at[slot]))

    @pl.when(i < nu_ref[0])
    def _():
        slot = _stream_group_weights(i, f, pl.num_programs(0), gseq_ref, ng_ref, copies_of)
        x = x_ref[...].astype(BF16)
        g = jnp.dot(x, wbuf[slot, 0].astype(BF16), preferred_element_type=F32) + bg_ref[0]
        u = jnp.dot(x, wbuf[slot, 1].astype(BF16), preferred_element_type=F32) + bu_ref[0]
        gate = jnp.minimum(g, SWIGLU_LIMIT)
        up = jnp.clip(u, -SWIGLU_LIMIT, SWIGLU_LIMIT)
        act = (up + 1.0) * gate * _sigmoid(SWIGLU_ALPHA * gate)
        o_ref[...] = act.astype(BF16)

    @pl.when(i >= nu_ref[0])
    def _():
        o_ref[...] = jnp.zeros(o_ref.shape, BF16)


def _moe_gate_up(tables, xs, w_gu, b_gu):
    r, d = xs.shape
    nf = D_FF // MOE_TF
    grid_spec = pltpu.PrefetchScalarGridSpec(
        num_scalar_prefetch=len(tables),
        grid=(nf, r // MOE_TM),
        in_specs=[
            pl.BlockSpec((MOE_TM, d), lambda f, i, te, nu, *_: (jnp.minimum(i, nu[0] - 1), 0)),
            pl.BlockSpec(memory_space=pl.ANY),
            pl.BlockSpec((1, 1, MOE_TF), lambda f, i, te, *_: (te[i], 0, f)),
            pl.BlockSpec((1, 1, MOE_TF), lambda f, i, te, *_: (te[i], 0, nf + f)),
        ],
        out_specs=pl.BlockSpec((MOE_TM, MOE_TF), lambda f, i, *_: (i, f)),
        scratch_shapes=[pltpu.VMEM((2, 2, d, MOE_TF), F32), pltpu.SemaphoreType.DMA((2,))],
    )
    return pl.pallas_call(
        _moe_gu_kernel,
        out_shape=jax.ShapeDtypeStruct((r, D_FF), BF16),
        grid_spec=grid_spec,
        compiler_params=_cparams(("arbitrary", "arbitrary")),
        name="moe_gate_up",
    )(*tables, xs, w_gu, b_gu, b_gu)


def _moe_down_kernel(te_ref, nu_ref, gseq_ref, gexp_ref, ng_ref, a_ref, w_hbm, bd_ref, o_ref,
                     wbuf, sem):
    n = pl.program_id(0)
    i = pl.program_id(1)

    def copies_of(g, slot):
        ch = g // ng_ref[0]
        e = gexp_ref[g - ch * ng_ref[0]]
        col = pl.multiple_of(ch * MOE_TN, MOE_TN)
        return (pltpu.make_async_copy(w_hbm.at[e, :, pl.ds(col, MOE_TN)],
                                      wbuf.at[slot], sem.at[slot]),)

    @pl.when(i < nu_ref[0])
    def _():
        slot = _stream_group_weights(i, n, pl.num_programs(0), gseq_ref, ng_ref, copies_of)
        o_ref[...] = (jnp.dot(a_ref[...], wbuf[slot].astype(BF16), preferred_element_type=F32)
                      + bd_ref[0])

    @pl.when(i >= nu_ref[0])
    def _():
        o_ref[...] = jnp.zeros(o_ref.shape, F32)


def _moe_down(tables, act, w_d, b_d):
    r = act.shape[0]
    nn = D_MODEL // MOE_TN
    grid_spec = pltpu.PrefetchScalarGridSpec(
        num_scalar_prefetch=len(tables),
        grid=(nn, r // MOE_TM),
        in_specs=[
            pl.BlockSpec((MOE_TM, D_FF), lambda n, i, te, nu, *_: (jnp.minimum(i, nu[0] - 1), 0)),
            pl.BlockSpec(memory_space=pl.ANY),
            pl.BlockSpec((1, 1, MOE_TN), lambda n, i, te, *_: (te[i], 0, n)),
        ],
        out_specs=pl.BlockSpec((MOE_TM, MOE_TN), lambda n, i, *_: (i, n)),
        scratch_shapes=[pltpu.VMEM((2, D_FF, MOE_TN), F32), pltpu.SemaphoreType.DMA((2,))],
    )
    return pl.pallas_call(
        _moe_down_kernel,
        out_shape=jax.ShapeDtypeStruct((r, D_MODEL), F32),
        grid_spec=grid_spec,
        compiler_params=_cparams(("arbitrary", "arbitrary")),
        name="moe_down",
    )(*tables, act, w_d, b_d)


def _final_kernel(pos_ref, x1_ref, w_ref, g2_ref, gf_ref, y_hbm, o_ref, buf, sem):
    m = pl.program_id(0)
    nm = pl.num_programs(0)
    tm = x1_ref.shape[0]

    def issue(step, slot):
        base = step * (tm * TOP_K)

        def body(t, carry):
            for k in range(TOP_K):
                r = pos_ref[base + t * TOP_K + k]
                pltpu.make_async_copy(y_hbm.at[pl.ds(r, 1), :],
                                      buf.at[slot, k, pl.ds(t, 1), :], sem.at[slot]).start()
            return carry

        lax.fori_loop(0, tm, body, 0, unroll=DMA_ISSUE_UNROLL)

    @pl.when(m == 0)
    def _():
        issue(0, 0)

    @pl.when(m + 1 < nm)
    def _():
        issue(m + 1, (m + 1) % 2)

    slot = m % 2
    for k in range(TOP_K):
        pltpu.make_async_copy(y_hbm.at[pl.ds(0, tm), :], buf.at[slot, k], sem.at[slot]).wait()
    w = w_ref[...]
    moe = w[:, 0:1] * buf[slot, 0]
    for k in range(1, TOP_K):
        moe = moe + w[:, k:k + 1] * buf[slot, k]
    x2 = x1_ref[...] + g2_ref[...] * moe
    o_ref[...] = x2 * lax.rsqrt(jnp.mean(x2 * x2, axis=-1, keepdims=True) + EPS) * gf_ref[...]


def _final(pos_flat, x1, w_top, g2, gf, ys):
    s, d = x1.shape
    tm = FIN_TM
    const = lambda m, pos: (0, 0)
    grid_spec = pltpu.PrefetchScalarGridSpec(
        num_scalar_prefetch=1,
        grid=(s // tm,),
        in_specs=[
            pl.BlockSpec((tm, d), lambda m, pos: (m, 0)),
            pl.BlockSpec((tm, LANES), lambda m, pos: (m, 0)),
            pl.BlockSpec((1, d), const),
            pl.BlockSpec((1, d), const),
            pl.BlockSpec(memory_space=pl.ANY),
        ],
        out_specs=pl.BlockSpec((tm, d), lambda m, pos: (m, 0)),
        scratch_shapes=[pltpu.VMEM((2, TOP_K, tm, d), F32), pltpu.SemaphoreType.DMA((2,))],
    )
    return pl.pallas_call(
        _final_kernel,
        out_shape=jax.ShapeDtypeStruct((s, d), F32),
        grid_spec=grid_spec,
        compiler_params=_cparams(("arbitrary",)),
        name="final_combine",
    )(pos_flat, x1, w_top, g2, gf, ys)


ROUTE_TM = 512
META_NUSED, META_PAD0, META_PADN = 0, 1, 2


def _route_kernel(lg_ref, pos_ref, w_ref, te_ref, nu_ref, cnt_s, pstart_s, carry_s):
    p = pl.program_id(0)
    m = pl.program_id(1)
    tm = lg_ref.shape[0]
    lane = lax.broadcasted_iota(jnp.int32, (tm, LANES), 1)
    lanef = lane.astype(F32)
    l = jnp.where(lane < N_EXPERTS, lg_ref[...], -jnp.inf)
    sel = jnp.zeros((tm, LANES), F32)
    vals, hots = [], []
    for _ in range(TOP_K):
        mx = jnp.max(l, axis=1, keepdims=True)
        idx = jnp.min(jnp.where(l == mx, lanef, float(LANES)), axis=1, keepdims=True)
        hot = lanef == idx
        vals.append(mx)
        hots.append(hot)
        l = jnp.where(hot, -jnp.inf, l)
        sel = sel + jnp.where(hot, 1.0, 0.0)
    colsum = jnp.sum(sel, axis=0, keepdims=True)

    @pl.when(p == 0)
    def _():
        @pl.when(m == 0)
        def _():
            cnt_s[...] = jnp.zeros(cnt_s.shape, F32)

        cnt_s[...] = cnt_s[...] + colsum

    @pl.when(jnp.logical_and(p == 1, m == 0))
    def _():
        cnt = cnt_s[...]
        padded = jnp.floor((cnt + (MOE_TM - 1.0)) * (1.0 / MOE_TM)) * MOE_TM
        lane8 = lax.broadcasted_iota(jnp.int32, cnt.shape, 1)
        pend = padded
        d = 1
        while d < LANES:
            pend = pend + jnp.where(lane8 >= d, pltpu.roll(pend, d, axis=1), 0.0)
            d *= 2
        pstart_s[...] = pend - padded
        carry_s[...] = jnp.zeros(carry_s.shape, F32)
        total = pend[:, N_EXPERTS - 1:N_EXPERTS]
        nt = te_ref.shape[0]
        tstart = lax.broadcasted_iota(jnp.int32, (nt, LANES), 0).astype(F32) * MOE_TM
        lane_t = lax.broadcasted_iota(jnp.int32, (nt, LANES), 1)
        done = jnp.where(lane_t < N_EXPERTS, jnp.where(tstart >= pend[0:1, :], 1.0, 0.0), 0.0)
        te = jnp.sum(done, axis=1, keepdims=True)
        last = jnp.sum(jnp.where(lane8 < N_EXPERTS,
                                 jnp.where(pend <= total - 1.0, 1.0, 0.0), 0.0),
                       axis=1, keepdims=True)
        te_ref[...] = jnp.minimum(te, last[0:1, :]).astype(jnp.int32)
        mrow = lax.broadcasted_iota(jnp.int32, cnt.shape, 0)
        meta = jnp.where(mrow == META_NUSED, jnp.broadcast_to(total * (1.0 / MOE_TM), cnt.shape),
                         jnp.where(mrow == META_PAD0, pend - padded + cnt, padded - cnt))
        nu_ref[...] = meta.astype(jnp.int32)

    @pl.when(p == 1)
    def _():
        r = lax.broadcasted_iota(jnp.int32, (tm, tm), 0)
        c = lax.broadcasted_iota(jnp.int32, (tm, tm), 1)
        tri = jnp.where(r > c, 1.0, 0.0).astype(BF16)
        rowbase = (jnp.dot(tri, sel.astype(BF16), preferred_element_type=F32)
                   + carry_s[0:1, :] + pstart_s[0:1, :])
        carry_s[...] = carry_s[...] + colsum
        den = jnp.exp(vals[0] - vals[0])
        for k in range(1, TOP_K):
            den = den + jnp.exp(vals[k] - vals[0])
        posf = jnp.zeros((tm, LANES), F32)
        wf = jnp.zeros((tm, LANES), F32)
        for k in range(TOP_K):
            pk = jnp.sum(jnp.where(hots[k], rowbase, 0.0), axis=1, keepdims=True)
            wk = jnp.exp(vals[k] - vals[0]) / den
            posf = jnp.where(lane == k, pk, posf)
            wf = jnp.where(lane == k, wk, wf)
        pos_ref[...] = posf.astype(jnp.int32)
        w_ref[...] = wf


def _route(lg, n_tiles):
    s = lg.shape[0]
    tm = ROUTE_TM
    nt_pad = -(-n_tiles // 8) * 8
    const = lambda p, m: (0, 0)
    return pl.pallas_call(
        _route_kernel,
        out_shape=(jax.ShapeDtypeStruct((s, LANES), jnp.int32),
                   jax.ShapeDtypeStruct((s, LANES), F32),
                   jax.ShapeDtypeStruct((nt_pad, 1), jnp.int32),
                   jax.ShapeDtypeStruct((8, LANES), jnp.int32)),
        grid=(2, s // tm),
        in_specs=[pl.BlockSpec((tm, LANES), lambda p, m: (m, 0))],
        out_specs=(pl.BlockSpec((tm, LANES), lambda p, m: (m * p, 0)),
                   pl.BlockSpec((tm, LANES), lambda p, m: (m * p, 0)),
                   pl.BlockSpec((nt_pad, 1), const),
                   pl.BlockSpec((8, LANES), const)),
        scratch_shapes=[pltpu.VMEM((8, LANES), F32), pltpu.VMEM((8, LANES), F32),
                        pltpu.VMEM((8, LANES), F32)],
        compiler_params=_cparams(("arbitrary", "arbitrary")),
        name="route",
    )(lg)


DISP_TM = 256


ZERO_ROWS = MOE_TM // 2


def _dispatch_kernel(pos_ref, pad0_ref, padn_ref, nu_ref, h_ref, xs_hbm, zero_s, sem, zsem):
    m = pl.program_id(0)
    tm = h_ref.shape[0]
    base = m * (tm * TOP_K)
    n_tiles = xs_hbm.shape[0] // MOE_TM

    def pad_copies(e, act):
        start = pad0_ref[e]
        n = padn_ref[e]
        head = jnp.minimum(n, (-start) & (SUBLANES - 1))

        def one_row(r, c):
            act(pltpu.make_async_copy(zero_s.at[pl.ds(0, 1), :],
                                      xs_hbm.at[pl.ds(start + r, 1), :], zsem))
            return c

        lax.fori_loop(0, head, one_row, 0)
        start = start + head
        n = n - head
        b = ZERO_ROWS
        while b >= SUBLANES:
            @pl.when((n & b) != 0)
            def _(b=b):
                off = pl.multiple_of(start + (n & ~(2 * b - 1)), SUBLANES)
                act(pltpu.make_async_copy(zero_s.at[pl.ds(0, b), :],
                                          xs_hbm.at[pl.ds(off, b), :], zsem))
            b //= 2

    def tail_copies(j, act):
        for half in range(MOE_TM // ZERO_ROWS):
            off = pl.multiple_of(j * MOE_TM + half * ZERO_ROWS, ZERO_ROWS)
            act(pltpu.make_async_copy(zero_s, xs_hbm.at[pl.ds(off, ZERO_ROWS), :], zsem))

    @pl.when(m == 0)
    def _():
        zero_s[...] = jnp.zeros(zero_s.shape, F32)
        for act in (lambda cp: cp.start(), lambda cp: cp.wait()):
            lax.fori_loop(0, N_EXPERTS, lambda e, c, act=act: (pad_copies(e, act), c)[1], 0)
            lax.fori_loop(nu_ref[0], n_tiles, lambda j, c, act=act: (tail_copies(j, act), c)[1], 0)

    def body(t, carry):
        for k in range(TOP_K):
            r = pos_ref[base + t * TOP_K + k]
            pltpu.make_async_copy(h_ref.at[pl.ds(t, 1), :], xs_hbm.at[pl.ds(r, 1), :], sem).start()
        return carry

    lax.fori_loop(0, tm, body, 0, unroll=DMA_ISSUE_UNROLL)
    for _ in range(TOP_K):
        pltpu.make_async_copy(h_ref, xs_hbm.at[pl.ds(0, tm), :], sem).wait()


def _dispatch(pos_flat, pad0, padn, n_used, h2, n_rows):
    s, d = h2.shape
    tm = DISP_TM
    grid_spec = pltpu.PrefetchScalarGridSpec(
        num_scalar_prefetch=4,
        grid=(s // tm,),
        in_specs=[pl.BlockSpec((tm, d), lambda m, *_: (m, 0))],
        out_specs=pl.BlockSpec(memory_space=pl.ANY),
        scratch_shapes=[pltpu.VMEM((ZERO_ROWS, d), F32), pltpu.SemaphoreType.DMA(()),
                        pltpu.SemaphoreType.DMA(())],
    )
    return pl.pallas_call(
        _dispatch_kernel,
        out_shape=jax.ShapeDtypeStruct((n_rows, d), F32),
        grid_spec=grid_spec,
        compiler_params=_cparams(("arbitrary",)),
        name="dispatch",
    )(pos_flat, pad0, padn, n_used, h2)


def _gate_weights_t(w_in_t):
    return jnp.concatenate([w_in_t[W_IN_IGATE:W_IN_IGATE + 2 * M_HEADS],
                            w_in_t[W_IN_FGATE:W_IN_FGATE + F_HEADS],
                            jnp.zeros((LANES - G_ROWS, w_in_t.shape[1]), F32)], axis=0)


def kernel(x, c, ada_w, ada_b, norm1_g, w_in, m_conv_w, m_conv_b, m_i_bias, m_f_bias,
           m_out_norm_g, f_f_bias, f_q_norm_g, f_k_norm_g, w_branch_m, w_branch_f, w_out,
           norm2_g, w_router, b_router, w_gate_up, b_gate_up, w_down, b_down, final_norm_g):
    b, s, d = x.shape
    assert b == 1 and ada_w.shape[0] == 1
    x2d = x.reshape(s, d)

    mod = _ada_mod(c.reshape(d, 1), ada_w.reshape(d, 6 * d), ada_b.reshape(1, 6 * d))
    sh1, sc1, g1, sh2, sc2, g2 = [mod[:, i * d:(i + 1) * d] for i in range(6)]

    w_in_t = w_in.reshape(d, -1).T
    wg2 = jnp.concatenate(_split_bf16(_gate_weights_t(w_in_t)), axis=0)
    h1, zg = _norm1(x2d, norm1_g.reshape(1, d), sc1, sh1, wg2)
    z = _inproj(h1, w_in_t, f_q_norm_g.reshape(1, F_W), f_k_norm_g.reshape(1, F_W))

    gate_bias = jnp.concatenate([m_i_bias.reshape(1, M_HEADS), m_f_bias.reshape(1, M_HEADS),
                                 f_f_bias.reshape(1, F_HEADS),
                                 jnp.zeros((1, LANES - G_ROWS), F32)], axis=1)
    gcol, grow = _gates(zg, gate_bias)

    hm = _mlstm(z, gcol, grow, m_conv_w.reshape(M_CONV, 2 * M_QK_W),
                m_conv_b.reshape(1, 2 * M_QK_W),
                jnp.broadcast_to(m_out_norm_g.reshape(M_V_W, 1), (M_V_W, LANES)))
    hf = _fox(z, gcol)

    wr = jnp.concatenate([w_router.reshape(d, N_EXPERTS),
                          jnp.zeros((d, LANES - N_EXPERTS), F32)], axis=1)
    br = jnp.concatenate([b_router.reshape(1, N_EXPERTS),
                          jnp.zeros((1, LANES - N_EXPERTS), F32)], axis=1)
    x1, h2, lg = _outproj(hm, hf, z, x2d,
                          w_branch_m.reshape(M_V_W, d).astype(BF16),
                          w_branch_f.reshape(F_W, d).astype(BF16),
                          w_out.reshape(d, d).astype(BF16),
                          g1, norm2_g.reshape(1, d), sc2, sh2,
                          jnp.concatenate(_split_bf16(wr), axis=1), br)

    n_rows = s * TOP_K + N_EXPERTS * MOE_TM
    n_tiles = n_rows // MOE_TM
    pos, w_top, te, meta = _route(lg, n_tiles)
    pos_flat = pos[:, :TOP_K].reshape(s * TOP_K)
    tile_e = te[:n_tiles, 0]
    n_used = meta[META_NUSED, :1]
    xs = _dispatch(pos_flat, meta[META_PAD0, :N_EXPERTS], meta[META_PADN, :N_EXPERTS], n_used,
                   h2, n_rows)
    tables = (tile_e, n_used) + _group_tables(tile_e, n_used)
    act = _moe_gate_up(tables, xs, w_gate_up.reshape(N_EXPERTS, d, 2 * D_FF),
                       b_gate_up.reshape(N_EXPERTS, 1, 2 * D_FF))
    ys = _moe_down(tables, act, w_down.reshape(N_EXPERTS, D_FF, d),
                   b_down.reshape(N_EXPERTS, 1, d))
    out = _final(pos_flat, x1, w_top, g2, final_norm_g.reshape(1, d), ys)
    return out.reshape(b, s, d)
```
